```python
import functools
import jax, jax.numpy as jnp
from jax import lax
import numpy as np

D_MODEL = 2048
BATCH = 4
SEQ = 4096
DEPTH = 2

GRID_W = 64
CTX_LEN = 256
N_EVEN = (DEPTH + 1) // 2
N_ODD = DEPTH // 2
N_MOD = 6
EPS = 1e-6

POOL_WINDOWS = (2, 4, 8, 16)
POOL_GROUPS = 4
D_POOL = D_MODEL // 2
POOL_GROUP_DIM = D_POOL // POOL_GROUPS
D_CONV = D_MODEL // 2
CONV_WIDTH = 31
D_IN_EVEN = D_POOL + 2 * D_CONV
D_MIX_EVEN = D_POOL + D_CONV

MLA_HEADS = 8
Q_LORA = 512
KV_LORA = 512
QK_NOPE = 128
QK_ROPE = 64
V_HEAD = 128
ROPE_FREQS = QK_ROPE // 4
ROPE_THETA = 10000.0
SM_SCALE = (QK_NOPE + QK_ROPE) ** -0.5
Q_BLOCK = 128
SGU_HEADS = 8
D_SGU = D_MODEL // 2
SGU_HEAD_DIM = D_SGU // SGU_HEADS
CHUNK = 128
COL_Q = Q_LORA
COL_KV = Q_LORA + KV_LORA
COL_KR = Q_LORA + KV_LORA + QK_ROPE
COL_U = COL_KR + D_SGU
D_IN_ODD = COL_U + D_SGU
D_MIX_ODD = MLA_HEADS * V_HEAD + D_SGU

D_FF = 5632
N_EXPERTS = 8
TOP_K = 2
D_FF_EXPERT = 7168

kernel_name = "hybrid_pool_conv_mla_sgu_moe_dit"


def rmsnorm(x, g):
    xf = x.astype(jnp.float32)
    y = xf * lax.rsqrt(jnp.mean(xf * xf, axis=-1, keepdims=True) + EPS)
    return (y * g.astype(jnp.float32)).astype(x.dtype)


def layernorm(x, g, b):
    xf = x.astype(jnp.float32)
    mu = jnp.mean(xf, axis=-1, keepdims=True)
    var = jnp.mean(jnp.square(xf - mu), axis=-1, keepdims=True)
    y = (xf - mu) * lax.rsqrt(var + EPS)
    return (y * g.astype(jnp.float32) + b.astype(jnp.float32)).astype(x.dtype)


def ada_params(cond, w, b):
    m = jnp.matmul(jax.nn.silu(cond), w) + b
    return jnp.split(m[..., None, :], N_MOD, axis=-1)


def modulate(h, shift, scale):
    return h * (1 + scale) + shift


def swiglu(h, w_gate, w_up, w_down):
    return jnp.matmul(jax.nn.silu(jnp.matmul(h, w_gate)) * jnp.matmul(h, w_up), w_down)


def moe_swiglu(h, router_w, router_b, w_gate, w_up, w_down):
    logits = (jnp.matmul(h, router_w) + router_b).astype(jnp.float32)
    top_v, top_i = lax.top_k(logits, TOP_K)
    top_p = jax.nn.softmax(top_v, axis=-1)
    gates = jnp.sum(jax.nn.one_hot(top_i, N_EXPERTS, dtype=jnp.float32) * top_p[..., None], axis=-2).astype(h.dtype)
    y = jnp.zeros_like(h)
    for e in range(N_EXPERTS):
        y = y + gates[..., e:e + 1] * swiglu(h, w_gate[e], w_up[e], w_down[e])
    return y


def multiscale_pool(u, pool_w, pool_scale):
    b_, l_, _ = u.shape
    uf = u.astype(jnp.float32)
    csum = jnp.concatenate([jnp.zeros_like(uf[:, :1]), jnp.cumsum(uf, axis=1)], axis=1)
    t = jnp.arange(l_)
    outs = []
    for g, w in enumerate(POOL_WINDOWS):
        lo = jnp.clip(t - w // 2, 0, l_)
        hi = jnp.clip(t - w // 2 + w, 0, l_)
        seg = csum[:, :, g * POOL_GROUP_DIM:(g + 1) * POOL_GROUP_DIM]
        s = jnp.take(seg, hi, axis=1) - jnp.take(seg, lo, axis=1)
        outs.append(s / (hi - lo).astype(jnp.float32)[None, :, None])
    pooled = jnp.concatenate(outs, axis=-1).astype(u.dtype) - u
    pooled = pooled.reshape(b_, l_, POOL_GROUPS, POOL_GROUP_DIM)
    mixed = jnp.einsum('blgc,gcd->blgd', pooled, pool_w).reshape(b_, l_, D_POOL)
    return mixed * pool_scale


def conformer_conv(a, gate, conv_w, conv_b, ln_g, ln_b):
    z = a * jax.nn.sigmoid(gate)
    z = lax.conv_general_dilated(z, conv_w[:, None, :], window_strides=(1,),
                                 padding=[(CONV_WIDTH // 2, CONV_WIDTH // 2)],
                                 dimension_numbers=('NWC', 'WIO', 'NWC'),
                                 feature_group_count=D_CONV) + conv_b
    return jax.nn.silu(layernorm(z, ln_g, ln_b))


def pool_conv_mixer(h, w_in, pool_w, pool_scale, conv_w, conv_b, ln_g, ln_b, w_out):
    p = jnp.matmul(h, w_in)
    u, a, g = jnp.split(p, [D_POOL, D_POOL + D_CONV], axis=-1)
    y = jnp.concatenate([multiscale_pool(u, pool_w, pool_scale),
                         conformer_conv(a, g, conv_w, conv_b, ln_g, ln_b)], axis=-1)
    return jnp.matmul(y, w_out)


def rope_tables(l_):
    rows = l_ // GRID_W
    row = jnp.repeat(jnp.arange(rows), GRID_W).astype(jnp.float32)
    col = jnp.tile(jnp.arange(GRID_W), rows).astype(jnp.float32)
    inv = ROPE_THETA ** (-jnp.arange(ROPE_FREQS, dtype=jnp.float32) / ROPE_FREQS)
    ang = jnp.stack([row[:, None] * inv, col[:, None] * inv], axis=1)
    return jnp.cos(ang), jnp.sin(ang)


def apply_rope_2d(x, cos, sin):
    shp = x.shape
    xr = x.reshape(shp[:-1] + (2, 2, ROPE_FREQS)).astype(jnp.float32)
    x1, x2 = xr[..., 0, :], xr[..., 1, :]
    out = jnp.stack([x1 * cos - x2 * sin, x2 * cos + x1 * sin], axis=-2)
    return out.reshape(shp).astype(x.dtype)


def mla_q(cq, q_norm_g, w_qb):
    q = jnp.matmul(rmsnorm(cq, q_norm_g), w_qb).reshape(cq.shape[:2] + (MLA_HEADS, QK_NOPE + QK_ROPE))
    return jnp.split(q, [QK_NOPE], axis=-1)


def mla_kv(ckv, kv_norm_g, w_kvb):
    kv = jnp.matmul(rmsnorm(ckv, kv_norm_g), w_kvb).reshape(ckv.shape[:2] + (MLA_HEADS, QK_NOPE + V_HEAD))
    return jnp.split(kv, [QK_NOPE], axis=-1)


def attend(q_nope, q_rope, k_nope, k_rope, v):
    s = (jnp.einsum('bqhd,bkhd->bhqk', q_nope, k_nope, preferred_element_type=jnp.float32)
         + jnp.einsum('bqhr,bkr->bhqk', q_rope, k_rope, preferred_element_type=jnp.float32)) * SM_SCALE
    p = jax.nn.softmax(s, axis=-1).astype(v.dtype)
    return jnp.einsum('bhqk,bkhd->bqhd', p, v)


def blocked_attend(q_nope, q_rope, k_nope, k_rope, v):
    b_, l_ = q_nope.shape[:2]
    nb = l_ // Q_BLOCK
    qn = q_nope.reshape(b_, nb, Q_BLOCK, MLA_HEADS, QK_NOPE).swapaxes(0, 1)
    qr = q_rope.reshape(b_, nb, Q_BLOCK, MLA_HEADS, QK_ROPE).swapaxes(0, 1)
    out = lax.map(lambda qs: attend(qs[0], qs[1], k_nope, k_rope, v), (qn, qr))
    return out.swapaxes(0, 1).reshape(b_, l_, MLA_HEADS * V_HEAD)


def chunk_sgu(zu, zv, ln_g, ln_b, sgu_w, sgu_b):
    b_, l_, _ = zu.shape
    v = layernorm(zv, ln_g, ln_b).reshape(b_, l_ // CHUNK, CHUNK, SGU_HEADS, SGU_HEAD_DIM)
    mixed = jnp.einsum('hij,bnjhd->bnihd', sgu_w, v) + sgu_b.T[None, None, :, :, None]
    return zu * mixed.reshape(b_, l_, D_SGU)


def mla_sgu_mixer(h, hc, w_in, q_norm_g, w_qb, kv_norm_g, w_kvb, sgu_ln_g, sgu_ln_b, sgu_w, sgu_b,
                  w_out, cos, sin, with_ctx_out):
    cols = (COL_Q, COL_KV, COL_KR, COL_U)
    cq, ckv, kr, zu, zv = jnp.split(jnp.matmul(h, w_in), cols, axis=-1)
    if with_ctx_out:
        cq_c, ckv_c, kr_c, zu_c, zv_c = jnp.split(jnp.matmul(hc, w_in), cols, axis=-1)
    else:
        ckv_c, kr_c = jnp.split(jnp.matmul(hc, w_in[:, COL_Q:COL_KR]), [KV_LORA], axis=-1)
    kn_c, v_c = mla_kv(ckv_c, kv_norm_g, w_kvb)
    kn, v = mla_kv(ckv, kv_norm_g, w_kvb)
    kr = apply_rope_2d(kr, cos, sin)
    qn, qr = mla_q(cq, q_norm_g, w_qb)
    qr = apply_rope_2d(qr, cos[:, None], sin[:, None])
    k_nope = jnp.concatenate([kn_c, kn], axis=1)
    k_rope = jnp.concatenate([kr_c, kr], axis=1)
    v_all = jnp.concatenate([v_c, v], axis=1)
    attn = blocked_attend(qn, qr, k_nope, k_rope, v_all)
    sg = chunk_sgu(jax.nn.gelu(zu), jax.nn.gelu(zv), sgu_ln_g, sgu_ln_b, sgu_w, sgu_b)
    out = jnp.matmul(jnp.concatenate([attn, sg], axis=-1), w_out)
    out_c = None
    if with_ctx_out:
        qn_c, qr_c = mla_q(cq_c, q_norm_g, w_qb)
        attn_c = attend(qn_c, qr_c, kn_c, kr_c, v_c).reshape(hc.shape[:2] + (MLA_HEADS * V_HEAD,))
        sg_c = chunk_sgu(jax.nn.gelu(zu_c), jax.nn.gelu(zv_c), sgu_ln_g, sgu_ln_b, sgu_w, sgu_b)
        out_c = jnp.matmul(jnp.concatenate([attn_c, sg_c], axis=-1), w_out)
    return out, out_c


def setup_inputs(seed: int = 0) -> dict:
    key = jax.random.key(seed)
    ks = iter(jax.random.split(key, 48))
    D = D_MODEL

    def nrm(shape, scale=1.0):
        return jax.random.normal(next(ks), shape, jnp.float32) * scale

    def gain(shape):
        return 1.0 + 0.02 * jax.random.normal(next(ks), shape, jnp.float32)

    return {
        "x": nrm((BATCH, SEQ, D)),
        "c": nrm((BATCH, D)),
        "ctx": nrm((BATCH, CTX_LEN, D)),
        "c_ctx": nrm((D,)),
        "ada_w": nrm((DEPTH, D, N_MOD * D), 0.5 * D ** -0.5),
        "ada_b": nrm((DEPTH, N_MOD * D), 0.02),
        "norm1_g": gain((DEPTH, D)),
        "norm2_g": gain((DEPTH, D)),
        "e_w_in": nrm((N_EVEN, D, D_IN_EVEN), D ** -0.5),
        "e_pool_w": nrm((N_EVEN, POOL_GROUPS, POOL_GROUP_DIM, POOL_GROUP_DIM), POOL_GROUP_DIM ** -0.5),
        "e_pool_scale": gain((N_EVEN, D_POOL)),
        "e_conv_w": nrm((N_EVEN, CONV_WIDTH, D_CONV), CONV_WIDTH ** -0.5),
        "e_conv_b": nrm((N_EVEN, D_CONV), 0.02),
        "e_conv_ln_g": gain((N_EVEN, D_CONV)),
        "e_conv_ln_b": nrm((N_EVEN, D_CONV), 0.02),
        "e_w_out": nrm((N_EVEN, D_MIX_EVEN, D), D_MIX_EVEN ** -0.5),
        "e_ffn_w_gate": nrm((N_EVEN, D, D_FF), D ** -0.5),
        "e_ffn_w_up": nrm((N_EVEN, D, D_FF), D ** -0.5),
        "e_ffn_w_down": nrm((N_EVEN, D_FF, D), D_FF ** -0.5),
        "o_w_in": nrm((N_ODD, D, D_IN_ODD), D ** -0.5),
        "o_q_norm_g": gain((N_ODD, Q_LORA)),
        "o_w_qb": nrm((N_ODD, Q_LORA, MLA_HEADS * (QK_NOPE + QK_ROPE)), Q_LORA ** -0.5),
        "o_kv_norm_g": gain((N_ODD, KV_LORA)),
        "o_w_kvb": nrm((N_ODD, KV_LORA, MLA_HEADS * (QK_NOPE + V_HEAD)), KV_LORA ** -0.5),
        "o_sgu_ln_g": gain((N_ODD, D_SGU)),
        "o_sgu_ln_b": nrm((N_ODD, D_SGU), 0.02),
        "o_sgu_w": nrm((N_ODD, SGU_HEADS, CHUNK, CHUNK), CHUNK ** -0.5),
        "o_sgu_b": gain((N_ODD, SGU_HEADS, CHUNK)),
        "o_w_out": nrm((N_ODD, D_MIX_ODD, D), D_MIX_ODD ** -0.5),
        "o_router_w": nrm((N_ODD, D, N_EXPERTS), D ** -0.5),
        "o_router_b": nrm((N_ODD, N_EXPERTS), 0.01),
        "o_exp_w_gate": nrm((N_ODD, N_EXPERTS, D, D_FF_EXPERT), D ** -0.5),
        "o_exp_w_up": nrm((N_ODD, N_EXPERTS, D, D_FF_EXPERT), D ** -0.5),
        "o_exp_w_down": nrm((N_ODD, N_EXPERTS, D_FF_EXPERT, D), D_FF_EXPERT ** -0.5),
        "final_norm_g": gain((D,)),
    }


def reference(x, c, ctx, c_ctx, ada_w, ada_b, norm1_g, norm2_g,
              e_w_in, e_pool_w, e_pool_scale, e_conv_w, e_conv_b, e_conv_ln_g, e_conv_ln_b, e_w_out,
              e_ffn_w_gate, e_ffn_w_up, e_ffn_w_down,
              o_w_in, o_q_norm_g, o_w_qb, o_kv_norm_g, o_w_kvb, o_sgu_ln_g, o_sgu_ln_b, o_sgu_w, o_sgu_b,
              o_w_out, o_router_w, o_router_b, o_exp_w_gate, o_exp_w_up, o_exp_w_down, final_norm_g):
    seq_len = x.shape[1]
    cos, sin = rope_tables(seq_len)
    for i in range(DEPTH):
        j = i // 2
        even = i % 2 == 0
        last = i == DEPTH - 1
        need_ctx = (not last) or (not even)
        sh1, sc1, g1, sh2, sc2, g2 = ada_params(c, ada_w[i], ada_b[i])
        h = modulate(rmsnorm(x, norm1_g[i]), sh1, sc1)
        if need_ctx:
            csh1, csc1, cg1, csh2, csc2, cg2 = ada_params(c_ctx, ada_w[i], ada_b[i])
            hc = modulate(rmsnorm(ctx, norm1_g[i]), csh1, csc1)
        if even:
            mix = functools.partial(pool_conv_mixer, w_in=e_w_in[j], pool_w=e_pool_w[j],
                                    pool_scale=e_pool_scale[j], conv_w=e_conv_w[j], conv_b=e_conv_b[j],
                                    ln_g=e_conv_ln_g[j], ln_b=e_conv_ln_b[j], w_out=e_w_out[j])
            out = mix(h)
            out_c = None if last else mix(hc)
            ffn = functools.partial(swiglu, w_gate=e_ffn_w_gate[j], w_up=e_ffn_w_up[j],
                                    w_down=e_ffn_w_down[j])
        else:
            out, out_c = mla_sgu_mixer(h, hc, o_w_in[j], o_q_norm_g[j], o_w_qb[j], o_kv_norm_g[j],
                                       o_w_kvb[j], o_sgu_ln_g[j], o_sgu_ln_b[j], o_sgu_w[j], o_sgu_b[j],
                                       o_w_out[j], cos, sin, not last)
            ffn = functools.partial(moe_swiglu, router_w=o_router_w[j], router_b=o_router_b[j],
                                    w_gate=o_exp_w_gate[j], w_up=o_exp_w_up[j], w_down=o_exp_w_down[j])
        x = x + g1 * out
        x = x + g2 * ffn(modulate(rmsnorm(x, norm2_g[i]), sh2, sc2))
        if not last:
            ctx = ctx + cg1 * out_c
            ctx = ctx + cg2 * ffn(modulate(rmsnorm(ctx, norm2_g[i]), csh2, csc2))
    return rmsnorm(x, final_norm_g)
```

```python
import functools

import jax
import jax.numpy as jnp
from jax import lax
from jax.experimental import pallas as pl
from jax.experimental.pallas import tpu as pltpu

F32 = jnp.float32
BF16 = jnp.bfloat16

EPS = 1e-6
N_MOD = 6
GRID_W = 64
POOL_WINDOWS = (2, 4, 8, 16)
CONV_WIDTH = 31
MLA_HEADS = 8
Q_LORA = 512
KV_LORA = 512
QK_NOPE = 128
QK_ROPE = 64
V_HEAD = 128
ROPE_FREQS = QK_ROPE // 4
ROPE_THETA = 10000.0
SM_SCALE = (QK_NOPE + QK_ROPE) ** -0.5
SGU_HEADS = 8
CHUNK = 128
N_EXPERTS = 8

LANES = 128
HALO = 16
VMEM_LIMIT = 56 * 1024 * 1024
HEAD_PAD = 256
ROW_TILE = 256
MM_TILE = 512
FFN_TF = 512
MOE_TM = 512
MOE_TF = 1024
NEG = -1e30


def _params(sem):
    return pltpu.CompilerParams(dimension_semantics=sem, vmem_limit_bytes=VMEM_LIMIT)


def _resident(shape):
    nd = len(shape)
    return pl.BlockSpec(shape, lambda *_: (0,) * nd, pipeline_mode=pl.Buffered(1))


def _rmsnorm_mod(x, g, shift, scale):
    ms = jnp.mean(x * x, axis=-1, keepdims=True)
    return (x * lax.rsqrt(ms + EPS) * g) * (1.0 + scale) + shift


def _layernorm(x, g, b):
    mu = jnp.mean(x, axis=-1, keepdims=True)
    xc = x - mu
    var = jnp.mean(xc * xc, axis=-1, keepdims=True)
    return xc * lax.rsqrt(var + EPS) * g + b


def _silu(x):
    return x * jax.nn.sigmoid(x)


def _ada_kernel(c_ref, w_ref, b_ref, o_ref):
    s = _silu(c_ref[...]).astype(BF16)
    o_ref[0] = jnp.dot(s, w_ref[0].astype(BF16), preferred_element_type=F32) + b_ref[0]


def _ada(cond, ada_w, ada_b, tn=1024):
    depth, d, n = ada_w.shape
    rows = cond.shape[0]
    return pl.pallas_call(
        _ada_kernel,
        grid=(depth, n // tn),
        in_specs=[pl.BlockSpec((rows, d), lambda l, j: (0, 0)),
                  pl.BlockSpec((1, d, tn), lambda l, j: (l, 0, j)),
                  pl.BlockSpec((1, 1, tn), lambda l, j: (l, 0, j))],
        out_specs=pl.BlockSpec((1, rows, tn), lambda l, j: (l, 0, j)),
        out_shape=jax.ShapeDtypeStruct((depth, rows, n), F32),
        compiler_params=_params(("arbitrary", "arbitrary")),
        name="ada",
    )(cond, ada_w, ada_b.reshape(depth, 1, n))


def _mod_spec(layer, tm, n_lat_rows, seq, n_batch, rows_per_layer):
    def index(i, *_):
        r = jnp.where(i * tm < n_lat_rows, (i * tm) // seq, n_batch)
        return (layer * rows_per_layer + r, 0, 0)
    return index


def _inproj_kernel(x_ref, mod_ref, g_ref, w_ref, o_ref):
    h = _rmsnorm_mod(x_ref[...], g_ref[...], mod_ref[0, 0:1, :], mod_ref[0, 1:2, :])
    o_ref[...] = jnp.dot(h.astype(BF16), w_ref[...], preferred_element_type=F32).astype(o_ref.dtype)


def _inproj(xs, mods, mod_index, g, w, tm=MM_TILE):
    t, d = xs.shape
    n = w.shape[1]
    return pl.pallas_call(
        _inproj_kernel,
        grid=(t // tm,),
        in_specs=[pl.BlockSpec((tm, d), lambda i: (i, 0)),
                  pl.BlockSpec((1, N_MOD, d), mod_index),
                  _resident((1, d)),
                  _resident((d, n))],
        out_specs=pl.BlockSpec((tm, n), lambda i: (i, 0)),
        out_shape=jax.ShapeDtypeStruct((t, n), BF16),
        compiler_params=_params(("arbitrary",)),
        name="inproj0",
    )(xs, mods, g.reshape(1, d), w)


def _poolconv_kernel(up_ref, um_ref, un_ref, ap_ref, am_ref, an_ref, gp_ref, gm_ref, gn_ref,
                     pw_ref, ps_ref, cw_ref, cb_ref, lg_ref, lb_ref, y_ref,
                     ubuf, zbuf, cbuf, *, n_lat_tiles, seq, ctx_len):
    i = pl.program_id(0)
    tm = um_ref.shape[0]
    dp = um_ref.shape[1]
    is_lat = i < n_lat_tiles
    pos0 = jnp.where(is_lat, (i * tm) % seq, ((i - n_lat_tiles) * tm) % ctx_len)
    length = jnp.where(is_lat, seq, ctx_len)
    keep_p = jnp.where(pos0 == 0, 0.0, 1.0).astype(F32)
    keep_n = jnp.where(pos0 + tm == length, 0.0, 1.0).astype(F32)

    def glu(a_ref, g_ref):
        return a_ref[...].astype(F32) * jax.nn.sigmoid(g_ref[...].astype(F32))

    ubuf[0:HALO, :] = up_ref[...].astype(F32) * keep_p
    ubuf[HALO:HALO + tm, :] = um_ref[...].astype(F32)
    ubuf[HALO + tm:, :] = un_ref[...].astype(F32) * keep_n
    zbuf[0:HALO, :] = glu(ap_ref, gp_ref) * keep_p
    zbuf[HALO:HALO + tm, :] = glu(am_ref, gm_ref)
    zbuf[HALO + tm:, :] = glu(an_ref, gn_ref) * keep_n

    pos = pos0 + lax.broadcasted_iota(jnp.int32, (tm, 1), 0)
    gdim = dp // len(POOL_WINDOWS)
    for g, w in enumerate(POOL_WINDOWS):
        cols = slice(g * gdim, (g + 1) * gdim)
        s = ubuf[HALO - w // 2:HALO - w // 2 + tm, cols]
        for o in range(-w // 2 + 1, w // 2):
            s = s + ubuf[HALO + o:HALO + o + tm, cols]
        cnt = jnp.minimum(pos - w // 2 + w, length) - jnp.maximum(pos - w // 2, 0)
        pooled = s / cnt.astype(F32) - ubuf[HALO:HALO + tm, cols]
        mixed = jnp.dot(pooled.astype(BF16), pw_ref[g], preferred_element_type=F32)
        y_ref[:, cols] = (mixed * ps_ref[:, cols]).astype(y_ref.dtype)

    rb = 32
    half = CONV_WIDTH // 2
    for r in range(0, tm, rb):
        acc = cw_ref[0:1, :] * zbuf[HALO - half + r:HALO - half + r + rb, :]
        for k in range(1, CONV_WIDTH):
            acc = acc + cw_ref[k:k + 1, :] * zbuf[HALO - half + k + r:HALO - half + k + r + rb, :]
        cbuf[r:r + rb, :] = acc
    conv = _layernorm(cbuf[...] + cb_ref[...], lg_ref[...], lb_ref[...])
    y_ref[:, dp:] = _silu(conv).astype(y_ref.dtype)


def _poolconv(p, pool_w, pool_scale, conv_w, conv_b, ln_g, ln_b, n_lat_rows, seq, ctx_len,
              tm=ROW_TILE):
    t = p.shape[0]
    dp = pool_scale.shape[0]
    dc = conv_b.shape[0]
    hb = tm // HALO
    last_hb = t // HALO - 1

    def main(c):
        return pl.BlockSpec((tm, dp), lambda i: (i, c))

    def prev(c):
        return pl.BlockSpec((HALO, dp), lambda i: (jnp.maximum(i * hb - 1, 0), c))

    def nxt(c):
        return pl.BlockSpec((HALO, dp), lambda i: (jnp.minimum((i + 1) * hb, last_hb), c))

    kern = functools.partial(_poolconv_kernel, n_lat_tiles=n_lat_rows // tm, seq=seq, ctx_len=ctx_len)
    return pl.pallas_call(
        kern,
        grid=(t // tm,),
        in_specs=[prev(0), main(0), nxt(0), prev(1), main(1), nxt(1), prev(2), main(2), nxt(2),
                  _resident(pool_w.shape), _resident((1, dp)), _resident(conv_w.shape),
                  _resident((1, dc)), _resident((1, dc)), _resident((1, dc))],
        out_specs=pl.BlockSpec((tm, dp + dc), lambda i: (i, 0)),
        out_shape=jax.ShapeDtypeStruct((t, dp + dc), BF16),
        scratch_shapes=[pltpu.VMEM((tm + 2 * HALO, dp), F32),
                        pltpu.VMEM((tm + 2 * HALO, dc), F32),
                        pltpu.VMEM((tm, dc), F32)],
        compiler_params=_params(("arbitrary",)),
        name="poolconv",
    )(p, p, p, p, p, p, p, p, p, pool_w, pool_scale.reshape(1, dp), conv_w,
      conv_b.reshape(1, dc), ln_g.reshape(1, dc), ln_b.reshape(1, dc))


def _outproj_kernel(*refs, n_in, router):
    y_refs = refs[:n_in]
    w_ref, x_ref, mod_ref, g_ref = refs[n_in:n_in + 4]
    rest = refs[n_in + 4:]
    if router:
        rw_ref, rb_ref, xo_ref, ho_ref, r_ref = rest
    else:
        xo_ref, ho_ref = rest
    o = None
    k0 = 0
    for y_ref in y_refs:
        kk = y_ref.shape[1]
        part = jnp.dot(y_ref[...], w_ref[k0:k0 + kk, :], preferred_element_type=F32)
        o = part if o is None else o + part
        k0 += kk
    x1 = x_ref[...] + mod_ref[0, 2:3, :] * o
    xo_ref[...] = x1
    h = _rmsnorm_mod(x1, g_ref[...], mod_ref[0, 3:4, :], mod_ref[0, 4:5, :])
    ho_ref[...] = h.astype(ho_ref.dtype)
    if router:
        logits = jnp.dot(h.astype(BF16), rw_ref[...], preferred_element_type=F32) + rb_ref[...]
        lane = lax.broadcasted_iota(jnp.int32, logits.shape, 1)
        m1 = jnp.max(logits, axis=-1, keepdims=True)
        i1 = jnp.min(jnp.where(logits == m1, lane, LANES), axis=-1, keepdims=True)
        rest_l = jnp.where(lane == i1, NEG * 2, logits)
        m2 = jnp.max(rest_l, axis=-1, keepdims=True)
        i2 = jnp.min(jnp.where(rest_l == m2, lane, LANES), axis=-1, keepdims=True)
        e2 = jnp.exp(m2 - m1)
        p1 = 1.0 / (1.0 + e2)
        p2 = e2 / (1.0 + e2)
        r = jnp.where(lane == 0, i1.astype(F32), 0.0)
        r = jnp.where(lane == 1, i2.astype(F32), r)
        r = jnp.where(lane == 2, p1, r)
        r = jnp.where(lane == 3, p2, r)
        r_ref[...] = r


def _outproj(ys, w, xs, mods, mod_index, g, n_rows, h_dtype, router=None, tm=MM_TILE):
    d = xs.shape[1]
    in_specs = [pl.BlockSpec((tm, y.shape[1]), lambda i: (i, 0)) for y in ys]
    in_specs += [_resident(w.shape), pl.BlockSpec((tm, d), lambda i: (i, 0)),
                 pl.BlockSpec((1, N_MOD, d), mod_index), _resident((1, d))]
    args = list(ys) + [w, xs, mods, g.reshape(1, d)]
    out_specs = [pl.BlockSpec((tm, d), lambda i: (i, 0)), pl.BlockSpec((tm, d), lambda i: (i, 0))]
    out_shape = [jax.ShapeDtypeStruct((n_rows, d), F32), jax.ShapeDtypeStruct((n_rows, d), h_dtype)]
    if router is not None:
        rw, rb = router
        in_specs += [_resident(rw.shape), _resident(rb.shape)]
        args += [rw, rb]
        out_specs.append(pl.BlockSpec((tm, LANES), lambda i: (i, 0)))
        out_shape.append(jax.ShapeDtypeStruct((n_rows, LANES), F32))
    kern = functools.partial(_outproj_kernel, n_in=len(ys), router=router is not None)
    return pl.pallas_call(
        kern,
        grid=(n_rows // tm,),
        in_specs=in_specs,
        out_specs=out_specs,
        out_shape=out_shape,
        compiler_params=_params(("arbitrary",)),
        name="outproj_router" if router is not None else "outproj",
    )(*args)


def _ffn_kernel(h_ref, wg_ref, wu_ref, wd_ref, x_ref, mod_ref, o_ref, acc_ref):
    f = pl.program_id(1)
    h = h_ref[...]
    a = jnp.dot(h, wg_ref[...], preferred_element_type=F32)
    b = jnp.dot(h, wu_ref[...], preferred_element_type=F32)
    part = jnp.dot((_silu(a) * b).astype(BF16), wd_ref[...], preferred_element_type=F32)

    @pl.when(f == 0)
    def _():
        acc_ref[...] = part

    @pl.when(f > 0)
    def _():
        acc_ref[...] += part

    @pl.when(f == pl.num_programs(1) - 1)
    def _():
        o_ref[...] = x_ref[...] + mod_ref[0, 5:6, :] * acc_ref[...]


def _ffn(h, wg, wu, wd, xs, mods, mod_index, tm=MM_TILE, tf=FFN_TF):
    t, d = xs.shape
    ff = wg.shape[1]
    return pl.pallas_call(
        _ffn_kernel,
        grid=(t // tm, ff // tf),
        in_specs=[pl.BlockSpec((tm, d), lambda i, f: (i, 0)),
                  pl.BlockSpec((d, tf), lambda i, f: (0, f)),
                  pl.BlockSpec((d, tf), lambda i, f: (0, f)),
                  pl.BlockSpec((tf, d), lambda i, f: (f, 0)),
                  pl.BlockSpec((tm, d), lambda i, f: (i, 0)),
                  pl.BlockSpec((1, N_MOD, d), mod_index)],
        out_specs=pl.BlockSpec((tm, d), lambda i, f: (i, 0)),
        out_shape=jax.ShapeDtypeStruct((t, d), F32),
        scratch_shapes=[pltpu.VMEM((tm, d), F32)],
        compiler_params=_params(("arbitrary", "arbitrary")),
        name="ffn",
    )(h, wg, wu, wd, xs, mods)


def _rope(x, c, s):
    lane = lax.broadcasted_iota(jnp.int32, x.shape, 1)
    partner = jnp.where(lane % 32 < 16, pltpu.roll(x, LANES - 16, 1), pltpu.roll(x, 16, 1))
    return x * c + partner * s


def _mla_kernel(x_ref, mod_ref, g_ref, win_ref, qg_ref, wq_ref, kvg_ref, wkv_ref,
                cos_ref, sin_ref, slg_ref, slb_ref, sw_ref, sb_ref,
                q_ref, k_ref, v_ref, sg_ref, *, n_lat_tiles):
    i = pl.program_id(0)
    tm = x_ref.shape[0]
    h = _rmsnorm_mod(x_ref[...], g_ref[...], mod_ref[0, 0:1, :], mod_ref[0, 1:2, :]).astype(BF16)
    p = jnp.dot(h, win_ref[...], preferred_element_type=F32)
    c0 = Q_LORA
    c1 = c0 + KV_LORA
    d_sgu = slg_ref.shape[1]
    c2 = c1 + d_sgu
    c3 = c2 + d_sgu
    cos = cos_ref[...]
    sin = sin_ref[...]

    def rms(x, g):
        return x * lax.rsqrt(jnp.mean(x * x, axis=-1, keepdims=True) + EPS) * g

    ckv = rms(p[:, c0:c1], kvg_ref[...]).astype(BF16)
    kv = jnp.dot(ckv, wkv_ref[...], preferred_element_type=F32)
    kr = _rope(p[:, c3:c3 + LANES], cos, sin).astype(BF16)
    nk = MLA_HEADS * QK_NOPE
    for hd in range(MLA_HEADS):
        k_ref[:, hd * HEAD_PAD:hd * HEAD_PAD + QK_NOPE] = kv[:, hd * QK_NOPE:(hd + 1) * QK_NOPE].astype(BF16)
        k_ref[:, hd * HEAD_PAD + QK_NOPE:(hd + 1) * HEAD_PAD] = kr
    v_ref[...] = kv[:, nk:].astype(BF16)

    @pl.when(i < n_lat_tiles)
    def _():
        cq = rms(p[:, :c0], qg_ref[...]).astype(BF16)
        q = jnp.dot(cq, wq_ref[...], preferred_element_type=F32) * SM_SCALE
        for hd in range(MLA_HEADS):
            b0 = hd * HEAD_PAD
            q_ref[:, b0:b0 + QK_NOPE] = q[:, b0:b0 + QK_NOPE].astype(BF16)
            q_ref[:, b0 + QK_NOPE:b0 + HEAD_PAD] = _rope(q[:, b0 + QK_NOPE:b0 + HEAD_PAD], cos, sin).astype(BF16)

        zu = jax.nn.gelu(p[:, c1:c2])
        zv = _layernorm(jax.nn.gelu(p[:, c2:c3]), slg_ref[...], slb_ref[...]).astype(BF16)
        hdim = d_sgu // SGU_HEADS
        for ck in range(tm // CHUNK):
            rows = slice(ck * CHUNK, (ck + 1) * CHUNK)
            for hd in range(SGU_HEADS):
                cols = slice(hd * hdim, (hd + 1) * hdim)
                mixed = jnp.dot(sw_ref[hd], zv[rows, cols], preferred_element_type=F32) + sb_ref[:, cols]
                sg_ref[rows, cols] = (zu[rows, cols] * mixed).astype(BF16)


def _mla_proj(xs, mods, mod_index, g, w_in, qg, wq, kvg, wkv, cos_t, sin_t, slg, slb, sw, sb,
              n_lat_rows, seq, tm=ROW_TILE):
    t, d = xs.shape
    n_lat_tiles = n_lat_rows // tm
    seq_tiles = seq // tm
    d_sgu = slg.shape[0]

    def lat(i):
        return (jnp.minimum(i, n_lat_tiles - 1), 0)

    def table(i):
        return (jnp.where(i < n_lat_tiles, i % seq_tiles, seq_tiles), 0)

    kern = functools.partial(_mla_kernel, n_lat_tiles=n_lat_tiles)
    return pl.pallas_call(
        kern,
        grid=(t // tm,),
        in_specs=[pl.BlockSpec((tm, d), lambda i: (i, 0)),
                  pl.BlockSpec((1, N_MOD, d), mod_index),
                  _resident((1, d)), _resident(w_in.shape),
                  _resident((1, Q_LORA)), _resident(wq.shape),
                  _resident((1, KV_LORA)), _resident(wkv.shape),
                  pl.BlockSpec((tm, LANES), table), pl.BlockSpec((tm, LANES), table),
                  _resident((1, d_sgu)), _resident((1, d_sgu)), _resident(sw.shape), _resident(sb.shape)],
        out_specs=[pl.BlockSpec((tm, MLA_HEADS * HEAD_PAD), lat),
                   pl.BlockSpec((tm, MLA_HEADS * HEAD_PAD), lambda i: (i, 0)),
                   pl.BlockSpec((tm, MLA_HEADS * V_HEAD), lambda i: (i, 0)),
                   pl.BlockSpec((tm, d_sgu), lat)],
        out_shape=[jax.ShapeDtypeStruct((n_lat_rows, MLA_HEADS * HEAD_PAD), BF16),
                   jax.ShapeDtypeStruct((t, MLA_HEADS * HEAD_PAD), BF16),
                   jax.ShapeDtypeStruct((t, MLA_HEADS * V_HEAD), BF16),
                   jax.ShapeDtypeStruct((n_lat_rows, d_sgu), BF16)],
        compiler_params=_params(("arbitrary",)),
        name="mla_proj",
    )(xs, mods, g.reshape(1, d), w_in, qg.reshape(1, -1), wq, kvg.reshape(1, -1), wkv,
      cos_t, sin_t, slg.reshape(1, -1), slb.reshape(1, -1), sw, sb)


def _attn_kernel(q_ref, kc_ref, kl_ref, vc_ref, vl_ref, o_ref):
    q = q_ref[...]
    dn = (((1,), (1,)), ((), ()))
    sc = lax.dot_general(q, kc_ref[...], dn, preferred_element_type=F32)
    sl = lax.dot_general(q, kl_ref[...], dn, preferred_element_type=F32)
    m = jnp.maximum(jnp.max(sc, axis=-1, keepdims=True), jnp.max(sl, axis=-1, keepdims=True))
    pc = jnp.exp(sc - m)
    pq = jnp.exp(sl - m)
    den = jnp.sum(pc, axis=-1, keepdims=True) + jnp.sum(pq, axis=-1, keepdims=True)
    o = (jnp.dot(pc.astype(BF16), vc_ref[...], preferred_element_type=F32)
         + jnp.dot(pq.astype(BF16), vl_ref[...], preferred_element_type=F32))
    o_ref[...] = (o / den).astype(o_ref.dtype)


def _attention(q, k, v, n_batch, seq, ctx_len, tq=ROW_TILE):
    n_lat_rows = n_batch * seq
    qb = seq // tq
    ctx_blk0 = n_lat_rows // ctx_len
    return pl.pallas_call(
        _attn_kernel,
        grid=(n_batch, MLA_HEADS, qb),
        in_specs=[pl.BlockSpec((tq, HEAD_PAD), lambda b, h, j: (b * qb + j, h)),
                  pl.BlockSpec((ctx_len, HEAD_PAD), lambda b, h, j: (ctx_blk0 + b, h)),
                  pl.BlockSpec((seq, HEAD_PAD), lambda b, h, j: (b, h)),
                  pl.BlockSpec((ctx_len, V_HEAD), lambda b, h, j: (ctx_blk0 + b, h)),
                  pl.BlockSpec((seq, V_HEAD), lambda b, h, j: (b, h))],
        out_specs=pl.BlockSpec((tq, V_HEAD), lambda b, h, j: (b * qb + j, h)),
        out_shape=jax.ShapeDtypeStruct((n_lat_rows, MLA_HEADS * V_HEAD), BF16),
        compiler_params=_params(("arbitrary", "arbitrary", "arbitrary")),
        name="attention",
    )(q, k, k, v, v)


SRC_BITS = 15


def _moe_kernel(te_ref, nact_ref, route_ref, h_hbm, wg_ref, wu_ref, wd_ref, y_hbm,
                xbuf, xb, acc, gsem, ssem):
    i = pl.program_id(0)
    f = pl.program_id(1)
    nf = pl.num_programs(1)
    tm = xbuf.shape[0]
    active = i < nact_ref[0]
    base = i * tm

    def gather_copy(r):
        src = route_ref[base + r] & ((1 << SRC_BITS) - 1)
        return pltpu.make_async_copy(h_hbm.at[pl.ds(src, 1)], xbuf.at[pl.ds(r, 1)], gsem)

    def scatter_copy(r, dst):
        return pltpu.make_async_copy(acc.at[pl.ds(r, 1)], y_hbm.at[pl.ds(dst, 1)], ssem)

    def for_valid_rows(fn):
        def body(r, c):
            dst = (route_ref[base + r] >> SRC_BITS) - 1

            @pl.when(dst >= 0)
            def _():
                fn(scatter_copy(r, dst))
            return c
        lax.fori_loop(0, tm, body, 0)

    @pl.when(active & (f == 0))
    def _():
        def start(r, c):
            gather_copy(r).start()
            return c
        lax.fori_loop(0, tm, start, 0)

        def wait(r, c):
            gather_copy(r).wait()
            return c
        lax.fori_loop(0, tm, wait, 0)
        xb[...] = xbuf[...].astype(BF16)

    @pl.when(active)
    def _():
        x = xb[...]
        a = jnp.dot(x, wg_ref[0], preferred_element_type=F32)
        b = jnp.dot(x, wu_ref[0], preferred_element_type=F32)
        part = jnp.dot((_silu(a) * b).astype(BF16), wd_ref[0], preferred_element_type=F32)

        @pl.when(f == 0)
        def _():
            acc[...] = part

        @pl.when(f > 0)
        def _():
            acc[...] += part

    @pl.when(active & (f == nf - 1))
    def _():
        for_valid_rows(lambda cp: cp.start())
        for_valid_rows(lambda cp: cp.wait())


def _moe(h, wg, wu, wd, tile_expert, n_active, route, n_out_rows, tm=MOE_TM, tf=MOE_TF):
    d = h.shape[1]
    ff = wg.shape[2]
    n_tiles = tile_expert.shape[0]
    nf = ff // tf

    def fidx(i, f, te, nact, route):
        return jnp.where(i < nact[0], f, nf - 1)

    grid_spec = pltpu.PrefetchScalarGridSpec(
        num_scalar_prefetch=3,
        grid=(n_tiles, nf),
        in_specs=[pl.BlockSpec(memory_space=pl.ANY),
                  pl.BlockSpec((1, d, tf), lambda i, f, te, nact, route: (te[i], 0, fidx(i, f, te, nact, route))),
                  pl.BlockSpec((1, d, tf), lambda i, f, te, nact, route: (te[i], 0, fidx(i, f, te, nact, route))),
                  pl.BlockSpec((1, tf, d), lambda i, f, te, nact, route: (te[i], fidx(i, f, te, nact, route), 0))],
        out_specs=pl.BlockSpec(memory_space=pl.ANY),
        scratch_shapes=[pltpu.VMEM((tm, d), F32), pltpu.VMEM((tm, d), BF16), pltpu.VMEM((tm, d), F32),
                        pltpu.SemaphoreType.DMA, pltpu.SemaphoreType.DMA],
    )
    return pl.pallas_call(
        _moe_kernel,
        grid_spec=grid_spec,
        out_shape=jax.ShapeDtypeStruct((n_out_rows, d), F32),
        compiler_params=_params(("arbitrary", "arbitrary")),
        name="moe",
    )(tile_expert, n_active, route, h, wg, wu, wd)


def _route_plan(r, n_tok, tm):
    n_assign = 2 * n_tok
    e_flat = jnp.concatenate([r[:, 0], r[:, 1]]).astype(jnp.int32)
    counts = jnp.sum(e_flat[:, None] == jnp.arange(N_EXPERTS)[None, :], axis=0).astype(jnp.int32)
    order = jnp.argsort(e_flat, stable=True).astype(jnp.int32)
    gstart = jnp.cumsum(counts) - counts
    tiles_per = (counts + tm - 1) // tm
    tend = jnp.cumsum(tiles_per)
    tstart = tend - tiles_per
    n_active = tend[-1]
    n_tiles = n_assign // tm + N_EXPERTS
    j = jnp.arange(n_tiles, dtype=jnp.int32)
    te = jnp.minimum(jnp.sum(j[:, None] >= tend[None, :], axis=1), N_EXPERTS - 1).astype(jnp.int32)
    te = jnp.where(j < n_active, te, te[jnp.maximum(n_active - 1, 0)])
    rr = jnp.arange(n_tiles * tm, dtype=jnp.int32)
    jr = rr // tm
    er = te[jr]
    rin = (jr - tstart[er]) * tm + rr % tm
    valid = (jr < n_active) & (rin < counts[er])
    a = order[jnp.clip(gstart[er] + rin, 0, n_assign - 1)]
    src = jnp.where(valid, a % n_tok, 0)
    dst1 = jnp.where(valid, a + 1, 0)
    route = (src | (dst1 << SRC_BITS)).astype(jnp.int32)
    return te, n_active.reshape(1).astype(jnp.int32), route


def _combine_kernel(x_ref, y0_ref, y1_ref, r_ref, mod_ref, g_ref, o_ref):
    r = r_ref[...]
    y = r[:, 2:3] * y0_ref[...] + r[:, 3:4] * y1_ref[...]
    x = x_ref[...] + mod_ref[0, 5:6, :] * y
    ms = jnp.mean(x * x, axis=-1, keepdims=True)
    o_ref[...] = x * lax.rsqrt(ms + EPS) * g_ref[...]


def _combine(xs, y, r, mods, mod_index, g, tm=MM_TILE):
    t, d = xs.shape
    nb = t // tm
    return pl.pallas_call(
        _combine_kernel,
        grid=(nb,),
        in_specs=[pl.BlockSpec((tm, d), lambda i: (i, 0)),
                  pl.BlockSpec((tm, d), lambda i: (i, 0)),
                  pl.BlockSpec((tm, d), lambda i: (nb + i, 0)),
                  pl.BlockSpec((tm, LANES), lambda i: (i, 0)),
                  pl.BlockSpec((1, N_MOD, d), mod_index),
                  _resident((1, d))],
        out_specs=pl.BlockSpec((tm, d), lambda i: (i, 0)),
        out_shape=jax.ShapeDtypeStruct((t, d), F32),
        compiler_params=_params(("arbitrary",)),
        name="combine",
    )(xs, y, y, r, mods, g.reshape(1, d))


def _rope_tables(seq, extra_rows):
    rows = seq // GRID_W
    row = jnp.repeat(jnp.arange(rows), GRID_W).astype(F32)
    col = jnp.tile(jnp.arange(GRID_W), rows).astype(F32)
    inv = ROPE_THETA ** (-jnp.arange(ROPE_FREQS, dtype=F32) / ROPE_FREQS)
    ar = row[:, None] * inv
    ac = col[:, None] * inv
    cos = jnp.concatenate([jnp.cos(ar), jnp.cos(ar), jnp.cos(ac), jnp.cos(ac)], axis=1)
    sin = jnp.concatenate([-jnp.sin(ar), jnp.sin(ar), -jnp.sin(ac), jnp.sin(ac)], axis=1)
    pad = LANES - QK_ROPE
    cos = jnp.concatenate([cos, jnp.ones((seq, pad), F32)], axis=1)
    sin = jnp.concatenate([sin, jnp.zeros((seq, pad), F32)], axis=1)
    cos = jnp.concatenate([cos, jnp.ones((extra_rows, LANES), F32)], axis=0)
    sin = jnp.concatenate([sin, jnp.zeros((extra_rows, LANES), F32)], axis=0)
    return cos, sin


def kernel(x, c, ctx, c_ctx, ada_w, ada_b, norm1_g, norm2_g, e_w_in, e_pool_w, e_pool_scale, e_conv_w, e_conv_b, e_conv_ln_g, e_conv_ln_b, e_w_out, e_ffn_w_gate, e_ffn_w_up, e_ffn_w_down, o_w_in, o_q_norm_g, o_w_qb, o_kv_norm_g, o_w_kvb, o_sgu_ln_g, o_sgu_ln_b, o_sgu_w, o_sgu_b, o_w_out, o_router_w, o_router_b, o_exp_w_gate, o_exp_w_up, o_exp_w_down, final_norm_g):
    n_batch, seq, d = x.shape
    ctx_len = ctx.shape[1]
    n_lat = n_batch * seq
    n_ctx = n_batch * ctx_len
    assert ctx_len % ROW_TILE == 0 and seq % MM_TILE == 0 and n_ctx % MM_TILE == 0
    assert ada_w.shape[0] == 2 and e_w_in.shape[0] == 1 and o_w_in.shape[0] == 1

    rows_per_layer = 8 * ((n_batch + 1 + 7) // 8)
    cond = jnp.zeros((rows_per_layer, d), F32).at[:n_batch].set(c).at[n_batch].set(c_ctx)
    mods = _ada(cond, ada_w, ada_b).reshape(2 * rows_per_layer, N_MOD, d)

    def mod_index(layer, tm):
        return _mod_spec(layer, tm, n_lat, seq, n_batch, rows_per_layer)

    xs = jnp.concatenate([x.reshape(n_lat, d), ctx.reshape(n_ctx, d)], axis=0)

    p = _inproj(xs, mods, mod_index(0, MM_TILE), norm1_g[0], e_w_in[0].astype(BF16))
    y = _poolconv(p, e_pool_w[0].astype(BF16), e_pool_scale[0], e_conv_w[0], e_conv_b[0],
                  e_conv_ln_g[0], e_conv_ln_b[0], n_lat, seq, ctx_len)
    x1, h2 = _outproj([y], e_w_out[0].astype(BF16), xs, mods, mod_index(0, MM_TILE), norm2_g[0],
                      n_lat + n_ctx, BF16)
    x2 = _ffn(h2, e_ffn_w_gate[0].astype(BF16), e_ffn_w_up[0].astype(BF16),
              e_ffn_w_down[0].astype(BF16), x1, mods, mod_index(0, MM_TILE))

    w_in = o_w_in[0]
    c_kr = Q_LORA + KV_LORA
    c_u = c_kr + QK_ROPE
    w_in_r = jnp.concatenate(
        [w_in[:, :c_kr], w_in[:, c_u:], w_in[:, c_kr:c_u], jnp.zeros((d, LANES - QK_ROPE), F32)],
        axis=1).astype(BF16)
    wq = o_w_qb[0].reshape(Q_LORA, MLA_HEADS, QK_NOPE + QK_ROPE)
    wq = jnp.pad(wq, ((0, 0), (0, 0), (0, HEAD_PAD - QK_NOPE - QK_ROPE)))
    wq = wq.reshape(Q_LORA, MLA_HEADS * HEAD_PAD).astype(BF16)
    wkv = o_w_kvb[0].reshape(KV_LORA, MLA_HEADS, QK_NOPE + V_HEAD)
    wkv = jnp.concatenate([wkv[:, :, :QK_NOPE].reshape(KV_LORA, -1),
                           wkv[:, :, QK_NOPE:].reshape(KV_LORA, -1)], axis=1).astype(BF16)
    cos_t, sin_t = _rope_tables(seq, ROW_TILE)
    d_sgu = o_sgu_ln_g.shape[1]
    sgu_bias = jnp.repeat(o_sgu_b[0].T, d_sgu // SGU_HEADS, axis=1)
    q, k, v, sg = _mla_proj(x2, mods, mod_index(1, ROW_TILE), norm1_g[1], w_in_r,
                            o_q_norm_g[0], wq, o_kv_norm_g[0], wkv, cos_t, sin_t,
                            o_sgu_ln_g[0], o_sgu_ln_b[0], o_sgu_w[0].astype(BF16), sgu_bias,
                            n_lat, seq)
    attn = _attention(q, k, v, n_batch, seq, ctx_len)

    rw = jnp.pad(o_router_w[0], ((0, 0), (0, LANES - N_EXPERTS))).astype(BF16)
    rb = jnp.concatenate([o_router_b[0], jnp.full((LANES - N_EXPERTS,), NEG, F32)]).reshape(1, LANES)
    x3, h3, r = _outproj([attn, sg], o_w_out[0].astype(BF16), x2, mods, mod_index(1, MM_TILE),
                         norm2_g[1], n_lat, F32, router=(rw, rb))

    te, n_active, route = _route_plan(r, n_lat, MOE_TM)
    ys = _moe(h3, o_exp_w_gate[0].astype(BF16), o_exp_w_up[0].astype(BF16),
              o_exp_w_down[0].astype(BF16), te, n_active, route, 2 * n_lat)
    out = _combine(x3, ys, r, mods, mod_index(1, MM_TILE), final_norm_g)
    return out.reshape(n_batch, seq, d)
```

```python
import functools

import jax
import jax.numpy as jnp
from jax import lax
from jax.experimental import pallas as pl
from jax.experimental.pallas import tpu as pltpu

F32 = jnp.float32
BF16 = jnp.bfloat16

EPS = 1e-6
N_MOD = 6
GRID_W = 64
POOL_WINDOWS = (2, 4, 8, 16)
CONV_WIDTH = 31
MLA_HEADS = 8
Q_LORA = 512
KV_LORA = 512
QK_NOPE = 128
QK_ROPE = 64
V_HEAD = 128
ROPE_FREQS = QK_ROPE // 4
ROPE_THETA = 10000.0
SM_SCALE = (QK_NOPE + QK_ROPE) ** -0.5
SGU_HEADS = 8
CHUNK = 128
N_EXPERTS = 8

LANES = 128
HALO = 16
VMEM_LIMIT = 56 * 1024 * 1024
HEAD_PAD = 256
ROW_TILE = 256
MM_TILE = 512
FFN_TF = 512
MOE_TM = 512
MOE_NF = 7
NEG = -1e30


def _params(sem):
    return pltpu.CompilerParams(dimension_semantics=sem, vmem_limit_bytes=VMEM_LIMIT)


def _resident(shape):
    nd = len(shape)
    return pl.BlockSpec(shape, lambda *_: (0,) * nd, pipeline_mode=pl.Buffered(1))


def _rmsnorm_mod(x, g, shift, scale):
    ms = jnp.mean(x * x, axis=-1, keepdims=True)
    return (x * lax.rsqrt(ms + EPS) * g) * (1.0 + scale) + shift


def _layernorm(x, g, b):
    mu = jnp.mean(x, axis=-1, keepdims=True)
    xc = x - mu
    var = jnp.mean(xc * xc, axis=-1, keepdims=True)
    return xc * lax.rsqrt(var + EPS) * g + b


def _silu(x):
    return x * jax.nn.sigmoid(x)


def _ada_kernel(c_ref, w_ref, b_ref, o_ref):
    s = _silu(c_ref[...]).astype(BF16)
    o_ref[0] = jnp.dot(s, w_ref[0].astype(BF16), preferred_element_type=F32) + b_ref[0]


def _ada(cond, ada_w, ada_b, tn=1024):
    depth, d, n = ada_w.shape
    rows = cond.shape[0]
    return pl.pallas_call(
        _ada_kernel,
        grid=(depth, n // tn),
        in_specs=[pl.BlockSpec((rows, d), lambda l, j: (0, 0)),
                  pl.BlockSpec((1, d, tn), lambda l, j: (l, 0, j)),
                  pl.BlockSpec((1, 1, tn), lambda l, j: (l, 0, j))],
        out_specs=pl.BlockSpec((1, rows, tn), lambda l, j: (l, 0, j)),
        out_shape=jax.ShapeDtypeStruct((depth, rows, n), F32),
        compiler_params=_params(("arbitrary", "arbitrary")),
        name="ada",
    )(cond, ada_w, ada_b.reshape(depth, 1, n))


def _mod_spec(layer, tm, n_lat_rows, seq, n_batch, rows_per_layer):
    def index(i, *_):
        r = jnp.where(i * tm < n_lat_rows, (i * tm) // seq, n_batch)
        return (layer * rows_per_layer + r, 0, 0)
    return index


def _pick_rows(i, n_lat_tiles, lat_ref, ctx_ref):
    return jnp.where(i < n_lat_tiles, lat_ref[...], ctx_ref[...])


def _two_source_specs(tm, d, n_lat_tiles):
    return [pl.BlockSpec((tm, d), lambda i: (jnp.minimum(i, n_lat_tiles - 1), 0)),
            pl.BlockSpec((tm, d), lambda i: (jnp.maximum(i - n_lat_tiles, 0), 0))]


def _inproj_kernel(xl_ref, xc_ref, mod_ref, g_ref, w_ref, o_ref, *, n_lat_tiles):
    x = _pick_rows(pl.program_id(0), n_lat_tiles, xl_ref, xc_ref)
    h = _rmsnorm_mod(x, g_ref[...], mod_ref[0, 0:1, :], mod_ref[0, 1:2, :])
    o_ref[...] = jnp.dot(h.astype(BF16), w_ref[...], preferred_element_type=F32).astype(o_ref.dtype)


def _inproj(x_lat, x_ctx, mods, mod_index, g, w, tm=MM_TILE):
    d = x_lat.shape[1]
    t = x_lat.shape[0] + x_ctx.shape[0]
    n = w.shape[1]
    n_lat_tiles = x_lat.shape[0] // tm
    return pl.pallas_call(
        functools.partial(_inproj_kernel, n_lat_tiles=n_lat_tiles),
        grid=(t // tm,),
        in_specs=_two_source_specs(tm, d, n_lat_tiles) + [
            pl.BlockSpec((1, N_MOD, d), mod_index), _resident((1, d)), _resident((d, n))],
        out_specs=pl.BlockSpec((tm, n), lambda i: (i, 0)),
        out_shape=jax.ShapeDtypeStruct((t, n), BF16),
        compiler_params=_params(("arbitrary",)),
        name="inproj0",
    )(x_lat, x_ctx, mods, g.reshape(1, d), w)


def _poolconv_kernel(up_ref, um_ref, un_ref, ap_ref, am_ref, an_ref, gp_ref, gm_ref, gn_ref,
                     pw_ref, ps_ref, cw_ref, cb_ref, lg_ref, lb_ref, y_ref,
                     ubuf, zbuf, cbuf, *, n_lat_tiles, seq, ctx_len):
    i = pl.program_id(0)
    tm = um_ref.shape[0]
    dp = um_ref.shape[1]
    is_lat = i < n_lat_tiles
    pos0 = jnp.where(is_lat, (i * tm) % seq, ((i - n_lat_tiles) * tm) % ctx_len)
    length = jnp.where(is_lat, seq, ctx_len)
    keep_p = jnp.where(pos0 == 0, 0.0, 1.0).astype(F32)
    keep_n = jnp.where(pos0 + tm == length, 0.0, 1.0).astype(F32)

    def glu(a_ref, g_ref):
        return a_ref[...].astype(F32) * jax.nn.sigmoid(g_ref[...].astype(F32))

    ubuf[0:HALO, :] = up_ref[...].astype(F32) * keep_p
    ubuf[HALO:HALO + tm, :] = um_ref[...].astype(F32)
    ubuf[HALO + tm:, :] = un_ref[...].astype(F32) * keep_n
    zbuf[0:HALO, :] = glu(ap_ref, gp_ref) * keep_p
    zbuf[HALO:HALO + tm, :] = glu(am_ref, gm_ref)
    zbuf[HALO + tm:, :] = glu(an_ref, gn_ref) * keep_n

    pos = pos0 + lax.broadcasted_iota(jnp.int32, (tm, 1), 0)
    gdim = dp // len(POOL_WINDOWS)
    for g, w in enumerate(POOL_WINDOWS):
        cols = slice(g * gdim, (g + 1) * gdim)
        s = ubuf[HALO - w // 2:HALO - w // 2 + tm, cols]
        for o in range(-w // 2 + 1, w // 2):
            s = s + ubuf[HALO + o:HALO + o + tm, cols]
        cnt = jnp.minimum(pos - w // 2 + w, length) - jnp.maximum(pos - w // 2, 0)
        pooled = s / cnt.astype(F32) - ubuf[HALO:HALO + tm, cols]
        mixed = jnp.dot(pooled.astype(BF16), pw_ref[g], preferred_element_type=F32)
        y_ref[:, cols] = (mixed * ps_ref[:, cols]).astype(y_ref.dtype)

    rb = 32
    half = CONV_WIDTH // 2
    for r in range(0, tm, rb):
        acc = cw_ref[0:1, :] * zbuf[HALO - half + r:HALO - half + r + rb, :]
        for k in range(1, CONV_WIDTH):
            acc = acc + cw_ref[k:k + 1, :] * zbuf[HALO - half + k + r:HALO - half + k + r + rb, :]
        cbuf[r:r + rb, :] = acc
    conv = _layernorm(cbuf[...] + cb_ref[...], lg_ref[...], lb_ref[...])
    y_ref[:, dp:] = _silu(conv).astype(y_ref.dtype)


def _poolconv(p, pool_w, pool_scale, conv_w, conv_b, ln_g, ln_b, n_lat_rows, seq, ctx_len,
              tm=ROW_TILE):
    t = p.shape[0]
    dp = pool_scale.shape[0]
    dc = conv_b.shape[0]
    hb = tm // HALO
    last_hb = t // HALO - 1

    def main(c):
        return pl.BlockSpec((tm, dp), lambda i: (i, c))

    def prev(c):
        return pl.BlockSpec((HALO, dp), lambda i: (jnp.maximum(i * hb - 1, 0), c))

    def nxt(c):
        return pl.BlockSpec((HALO, dp), lambda i: (jnp.minimum((i + 1) * hb, last_hb), c))

    kern = functools.partial(_poolconv_kernel, n_lat_tiles=n_lat_rows // tm, seq=seq, ctx_len=ctx_len)
    return pl.pallas_call(
        kern,
        grid=(t // tm,),
        in_specs=[prev(0), main(0), nxt(0), prev(1), main(1), nxt(1), prev(2), main(2), nxt(2),
                  _resident(pool_w.shape), _resident((1, dp)), _resident(conv_w.shape),
                  _resident((1, dc)), _resident((1, dc)), _resident((1, dc))],
        out_specs=pl.BlockSpec((tm, dp + dc), lambda i: (i, 0)),
        out_shape=jax.ShapeDtypeStruct((t, dp + dc), BF16),
        scratch_shapes=[pltpu.VMEM((tm + 2 * HALO, dp), F32),
                        pltpu.VMEM((tm + 2 * HALO, dc), F32),
                        pltpu.VMEM((tm, dc), F32)],
        compiler_params=_params(("arbitrary",)),
        name="poolconv",
    )(p, p, p, p, p, p, p, p, p, pool_w, pool_scale.reshape(1, dp), conv_w,
      conv_b.reshape(1, dc), ln_g.reshape(1, dc), ln_b.reshape(1, dc))


def _outproj_kernel(*refs, n_in, n_res, n_lat_tiles, router):
    y_refs = refs[:n_in]
    w_ref = refs[n_in]
    x_refs = refs[n_in + 1:n_in + 1 + n_res]
    mod_ref, g_ref = refs[n_in + 1 + n_res:n_in + 3 + n_res]
    rest = refs[n_in + 3 + n_res:]
    if router:
        rw_ref, rb_ref, xo_ref, ho_ref, r_ref = rest
    else:
        xo_ref, ho_ref = rest
    o = None
    k0 = 0
    for y_ref in y_refs:
        kk = y_ref.shape[1]
        part = jnp.dot(y_ref[...], w_ref[k0:k0 + kk, :], preferred_element_type=F32)
        o = part if o is None else o + part
        k0 += kk
    if n_res == 2:
        x = _pick_rows(pl.program_id(0), n_lat_tiles, x_refs[0], x_refs[1])
    else:
        x = x_refs[0][...]
    x1 = x + mod_ref[0, 2:3, :] * o
    xo_ref[...] = x1
    h = _rmsnorm_mod(x1, g_ref[...], mod_ref[0, 3:4, :], mod_ref[0, 4:5, :])
    ho_ref[...] = h.astype(ho_ref.dtype)
    if router:
        logits = jnp.dot(h.astype(BF16), rw_ref[...], preferred_element_type=F32) + rb_ref[...]
        lane = lax.broadcasted_iota(jnp.int32, logits.shape, 1)
        m1 = jnp.max(logits, axis=-1, keepdims=True)
        i1 = jnp.min(jnp.where(logits == m1, lane, LANES), axis=-1, keepdims=True)
        rest_l = jnp.where(lane == i1, NEG * 2, logits)
        m2 = jnp.max(rest_l, axis=-1, keepdims=True)
        i2 = jnp.min(jnp.where(rest_l == m2, lane, LANES), axis=-1, keepdims=True)
        e2 = jnp.exp(m2 - m1)
        p1 = 1.0 / (1.0 + e2)
        p2 = e2 / (1.0 + e2)
        r = jnp.where(lane == 0, i1.astype(F32), 0.0)
        r = jnp.where(lane == 1, i2.astype(F32), r)
        r = jnp.where(lane == 2, p1, r)
        r = jnp.where(lane == 3, p2, r)
        r_ref[...] = r


def _outproj(ys, w, res, mods, mod_index, g, n_rows, h_dtype, router=None, tm=MM_TILE):
    d = w.shape[1]
    in_specs = [pl.BlockSpec((tm, y.shape[1]), lambda i: (i, 0)) for y in ys]
    in_specs.append(_resident(w.shape))
    n_lat_tiles = res[0].shape[0] // tm
    if len(res) == 2:
        in_specs += _two_source_specs(tm, d, n_lat_tiles)
    else:
        in_specs.append(pl.BlockSpec((tm, d), lambda i: (i, 0)))
    in_specs += [pl.BlockSpec((1, N_MOD, d), mod_index), _resident((1, d))]
    args = list(ys) + [w] + list(res) + [mods, g.reshape(1, d)]
    out_specs = [pl.BlockSpec((tm, d), lambda i: (i, 0)), pl.BlockSpec((tm, d), lambda i: (i, 0))]
    out_shape = [jax.ShapeDtypeStruct((n_rows, d), F32), jax.ShapeDtypeStruct((n_rows, d), h_dtype)]
    if router is not None:
        rw, rb = router
        in_specs += [_resident(rw.shape), _resident(rb.shape)]
        args += [rw, rb]
        out_specs.append(pl.BlockSpec((tm, LANES), lambda i: (i, 0)))
        out_shape.append(jax.ShapeDtypeStruct((n_rows, LANES), F32))
    kern = functools.partial(_outproj_kernel, n_in=len(ys), n_res=len(res), n_lat_tiles=n_lat_tiles,
                             router=router is not None)
    return pl.pallas_call(
        kern,
        grid=(n_rows // tm,),
        in_specs=in_specs,
        out_specs=out_specs,
        out_shape=out_shape,
        compiler_params=_params(("arbitrary",)),
        name="outproj_router" if router is not None else "outproj",
    )(*args)


def _ffn_kernel(h_ref, wg_ref, wu_ref, wd_ref, x_ref, mod_ref, o_ref, acc_ref):
    f = pl.program_id(1)
    h = h_ref[...]
    a = jnp.dot(h, wg_ref[...], preferred_element_type=F32)
    b = jnp.dot(h, wu_ref[...], preferred_element_type=F32)
    part = jnp.dot((_silu(a) * b).astype(BF16), wd_ref[...], preferred_element_type=F32)

    @pl.when(f == 0)
    def _():
        acc_ref[...] = part

    @pl.when(f > 0)
    def _():
        acc_ref[...] += part

    @pl.when(f == pl.num_programs(1) - 1)
    def _():
        o_ref[...] = x_ref[...] + mod_ref[0, 5:6, :] * acc_ref[...]


def _ffn(h, wg, wu, wd, xs, mods, mod_index, tm=MM_TILE, tf=FFN_TF):
    t, d = xs.shape
    ff = wg.shape[1]
    return pl.pallas_call(
        _ffn_kernel,
        grid=(t // tm, ff // tf),
        in_specs=[pl.BlockSpec((tm, d), lambda i, f: (i, 0)),
                  pl.BlockSpec((d, tf), lambda i, f: (0, f)),
                  pl.BlockSpec((d, tf), lambda i, f: (0, f)),
                  pl.BlockSpec((tf, d), lambda i, f: (f, 0)),
                  pl.BlockSpec((tm, d), lambda i, f: (i, 0)),
                  pl.BlockSpec((1, N_MOD, d), mod_index)],
        out_specs=pl.BlockSpec((tm, d), lambda i, f: (i, 0)),
        out_shape=jax.ShapeDtypeStruct((t, d), F32),
        scratch_shapes=[pltpu.VMEM((tm, d), F32)],
        compiler_params=_params(("arbitrary", "arbitrary")),
        name="ffn",
    )(h, wg, wu, wd, xs, mods)


def _rope(x, c, s):
    lane = lax.broadcasted_iota(jnp.int32, x.shape, 1)
    partner = jnp.where(lane % 32 < 16, pltpu.roll(x, LANES - 16, 1), pltpu.roll(x, 16, 1))
    return x * c + partner * s


def _mla_kernel(x_ref, mod_ref, g_ref, win_ref, qg_ref, wq_ref, kvg_ref, wkv_ref,
                cos_ref, sin_ref, slg_ref, slb_ref, sw_ref, sb_ref,
                q_ref, k_ref, v_ref, sg_ref, *, n_lat_tiles):
    i = pl.program_id(0)
    tm = x_ref.shape[0]
    h = _rmsnorm_mod(x_ref[...], g_ref[...], mod_ref[0, 0:1, :], mod_ref[0, 1:2, :]).astype(BF16)
    p = jnp.dot(h, win_ref[...], preferred_element_type=F32)
    c0 = Q_LORA
    c1 = c0 + KV_LORA
    d_sgu = slg_ref.shape[1]
    c2 = c1 + d_sgu
    c3 = c2 + d_sgu
    cos = cos_ref[...]
    sin = sin_ref[...]

    def rms(x, g):
        return x * lax.rsqrt(jnp.mean(x * x, axis=-1, keepdims=True) + EPS) * g

    ckv = rms(p[:, c0:c1], kvg_ref[...]).astype(BF16)
    kv = jnp.dot(ckv, wkv_ref[...], preferred_element_type=F32)
    kr = _rope(p[:, c3:c3 + LANES], cos, sin).astype(BF16)
    nk = MLA_HEADS * QK_NOPE
    for hd in range(MLA_HEADS):
        k_ref[:, hd * HEAD_PAD:hd * HEAD_PAD + QK_NOPE] = kv[:, hd * QK_NOPE:(hd + 1) * QK_NOPE].astype(BF16)
        k_ref[:, hd * HEAD_PAD + QK_NOPE:(hd + 1) * HEAD_PAD] = kr
    v_ref[...] = kv[:, nk:].astype(BF16)

    @pl.when(i < n_lat_tiles)
    def _():
        cq = rms(p[:, :c0], qg_ref[...]).astype(BF16)
        q = jnp.dot(cq, wq_ref[...], preferred_element_type=F32) * SM_SCALE
        for hd in range(MLA_HEADS):
            b0 = hd * HEAD_PAD
            q_ref[:, b0:b0 + QK_NOPE] = q[:, b0:b0 + QK_NOPE].astype(BF16)
            q_ref[:, b0 + QK_NOPE:b0 + HEAD_PAD] = _rope(q[:, b0 + QK_NOPE:b0 + HEAD_PAD], cos, sin).astype(BF16)

        zu = jax.nn.gelu(p[:, c1:c2])
        zv = _layernorm(jax.nn.gelu(p[:, c2:c3]), slg_ref[...], slb_ref[...]).astype(BF16)
        hdim = d_sgu // SGU_HEADS
        for ck in range(tm // CHUNK):
            rows = slice(ck * CHUNK, (ck + 1) * CHUNK)
            for hd in range(SGU_HEADS):
                cols = slice(hd * hdim, (hd + 1) * hdim)
                mixed = jnp.dot(sw_ref[hd], zv[rows, cols], preferred_element_type=F32) + sb_ref[:, cols]
                sg_ref[rows, cols] = (zu[rows, cols] * mixed).astype(BF16)


def _mla_proj(xs, mods, mod_index, g, w_in, qg, wq, kvg, wkv, cos_t, sin_t, slg, slb, sw, sb,
              n_lat_rows, seq, tm=ROW_TILE):
    t, d = xs.shape
    n_lat_tiles = n_lat_rows // tm
    seq_tiles = seq // tm
    d_sgu = slg.shape[0]

    def lat(i):
        return (jnp.minimum(i, n_lat_tiles - 1), 0)

    def table(i):
        return (jnp.where(i < n_lat_tiles, i % seq_tiles, seq_tiles), 0)

    kern = functools.partial(_mla_kernel, n_lat_tiles=n_lat_tiles)
    return pl.pallas_call(
        kern,
        grid=(t // tm,),
        in_specs=[pl.BlockSpec((tm, d), lambda i: (i, 0)),
                  pl.BlockSpec((1, N_MOD, d), mod_index),
                  _resident((1, d)), _resident(w_in.shape),
                  _resident((1, Q_LORA)), _resident(wq.shape),
                  _resident((1, KV_LORA)), _resident(wkv.shape),
                  pl.BlockSpec((tm, LANES), table), pl.BlockSpec((tm, LANES), table),
                  _resident((1, d_sgu)), _resident((1, d_sgu)), _resident(sw.shape), _resident(sb.shape)],
        out_specs=[pl.BlockSpec((tm, MLA_HEADS * HEAD_PAD), lat),
                   pl.BlockSpec((tm, MLA_HEADS * HEAD_PAD), lambda i: (i, 0)),
                   pl.BlockSpec((tm, MLA_HEADS * V_HEAD), lambda i: (i, 0)),
                   pl.BlockSpec((tm, d_sgu), lat)],
        out_shape=[jax.ShapeDtypeStruct((n_lat_rows, MLA_HEADS * HEAD_PAD), BF16),
                   jax.ShapeDtypeStruct((t, MLA_HEADS * HEAD_PAD), BF16),
                   jax.ShapeDtypeStruct((t, MLA_HEADS * V_HEAD), BF16),
                   jax.ShapeDtypeStruct((n_lat_rows, d_sgu), BF16)],
        compiler_params=_params(("arbitrary",)),
        name="mla_proj",
    )(xs, mods, g.reshape(1, d), w_in, qg.reshape(1, -1), wq, kvg.reshape(1, -1), wkv,
      cos_t, sin_t, slg.reshape(1, -1), slb.reshape(1, -1), sw, sb)


def _attn_kernel(q_ref, kc_ref, kl_ref, vc_ref, vl_ref, o_ref):
    q = q_ref[...]
    dn = (((1,), (1,)), ((), ()))
    sc = lax.dot_general(q, kc_ref[...], dn, preferred_element_type=F32)
    sl = lax.dot_general(q, kl_ref[...], dn, preferred_element_type=F32)
    m = jnp.maximum(jnp.max(sc, axis=-1, keepdims=True), jnp.max(sl, axis=-1, keepdims=True))
    pc = jnp.exp(sc - m)
    pq = jnp.exp(sl - m)
    den = jnp.sum(pc, axis=-1, keepdims=True) + jnp.sum(pq, axis=-1, keepdims=True)
    o = (jnp.dot(pc.astype(BF16), vc_ref[...], preferred_element_type=F32)
         + jnp.dot(pq.astype(BF16), vl_ref[...], preferred_element_type=F32))
    o_ref[...] = (o / den).astype(o_ref.dtype)


def _attention(q, k, v, n_batch, seq, ctx_len, tq=ROW_TILE):
    n_lat_rows = n_batch * seq
    qb = seq // tq
    ctx_blk0 = n_lat_rows // ctx_len
    return pl.pallas_call(
        _attn_kernel,
        grid=(n_batch, MLA_HEADS, qb),
        in_specs=[pl.BlockSpec((tq, HEAD_PAD), lambda b, h, j: (b * qb + j, h)),
                  pl.BlockSpec((ctx_len, HEAD_PAD), lambda b, h, j: (ctx_blk0 + b, h)),
                  pl.BlockSpec((seq, HEAD_PAD), lambda b, h, j: (b, h)),
                  pl.BlockSpec((ctx_len, V_HEAD), lambda b, h, j: (ctx_blk0 + b, h)),
                  pl.BlockSpec((seq, V_HEAD), lambda b, h, j: (b, h))],
        out_specs=pl.BlockSpec((tq, V_HEAD), lambda b, h, j: (b * qb + j, h)),
        out_shape=jax.ShapeDtypeStruct((n_lat_rows, MLA_HEADS * V_HEAD), BF16),
        compiler_params=_params(("arbitrary", "arbitrary", "arbitrary")),
        name="attention",
    )(q, k, k, v, v)


SRC_BITS = 15


def _moe_kernel(te_ref, nact_ref, route_ref, h_hbm, wg_ref, wu_ref, wd_ref, y_hbm,
                xbuf, xb, acc, gsem, ssem):
    i = pl.program_id(0)
    f = pl.program_id(1)
    tm = xbuf.shape[0]
    chunk = tm // (MOE_NF + 1)
    nact = nact_ref[0]
    active = i < nact
    slot = i % 2
    dump0 = y_hbm.shape[0] - tm

    def gather_copy(tile, r):
        src = route_ref[(tile + 1) * tm + r] & ((1 << SRC_BITS) - 1)
        return pltpu.make_async_copy(h_hbm.at[pl.ds(src, 1)], xbuf.at[pl.ds(r, 1)], gsem)

    def scatter_copy(tile, r):
        dst = route_ref[(tile + 1) * tm + r] >> SRC_BITS
        return pltpu.make_async_copy(acc.at[tile % 2, pl.ds(r, 1)], y_hbm.at[pl.ds(dst, 1)], ssem)

    def wait_gather():
        pltpu.make_async_copy(h_hbm.at[pl.ds(0, tm)], xbuf, gsem).wait()

    def wait_scatter():
        pltpu.make_async_copy(acc.at[0], y_hbm.at[pl.ds(0, tm)], ssem).wait()

    def start_all(copy, tile):
        def body(r, c):
            copy(tile, r).start()
            return c
        lax.fori_loop(0, tm, body, 0)

    @pl.when((i == 0) & (f == 0))
    def _():
        acc[1] = jnp.zeros(acc.shape[1:], F32)
        fill = pltpu.make_async_copy(acc.at[1], y_hbm.at[pl.ds(dump0, tm)], ssem)
        fill.start()
        fill.wait()
        start_all(gather_copy, 0)

    @pl.when((f == 0) & (i <= nact))
    def _():
        wait_gather()

        @pl.when(i >= 1)
        def _():
            wait_scatter()

    def issue_chunk(c):
        for j in range(chunk):
            r = c * chunk + j
            gather_copy(i + 1, r).start()
            scatter_copy(i - 1, r).start()

    @pl.when((f == 0) & active)
    def _():
        xb[...] = xbuf[...].astype(BF16)
        issue_chunk(0)

    @pl.when((f == 0) & (i == nact) & (i >= 1))
    def _():
        start_all(scatter_copy, i - 1)
        wait_scatter()

    @pl.when(active)
    def _():
        issue_chunk(f + 1)
        x = xb[...]
        a = jnp.dot(x, wg_ref[0], preferred_element_type=F32)
        b = jnp.dot(x, wu_ref[0], preferred_element_type=F32)
        part = jnp.dot((_silu(a) * b).astype(BF16), wd_ref[0], preferred_element_type=F32)

        @pl.when(f == 0)
        def _():
            acc[slot] = part

        @pl.when(f > 0)
        def _():
            acc[slot] += part


def _moe(h, wg, wu, wd, tile_expert, n_active, route, n_out_rows, tm=MOE_TM):
    d = h.shape[1]
    ff = wg.shape[2]
    n_tiles = tile_expert.shape[0]
    nf = MOE_NF
    tf = ff // nf

    def fidx(i, f, nact):
        return jnp.where(i < nact[0], f, nf - 1)

    grid_spec = pltpu.PrefetchScalarGridSpec(
        num_scalar_prefetch=3,
        grid=(n_tiles, nf),
        in_specs=[pl.BlockSpec(memory_space=pl.ANY),
                  pl.BlockSpec((1, d, tf), lambda i, f, te, nact, route: (te[i], 0, fidx(i, f, nact))),
                  pl.BlockSpec((1, d, tf), lambda i, f, te, nact, route: (te[i], 0, fidx(i, f, nact))),
                  pl.BlockSpec((1, tf, d), lambda i, f, te, nact, route: (te[i], fidx(i, f, nact), 0))],
        out_specs=pl.BlockSpec(memory_space=pl.ANY),
        scratch_shapes=[pltpu.VMEM((tm, d), F32), pltpu.VMEM((tm, d), BF16), pltpu.VMEM((2, tm, d), F32),
                        pltpu.SemaphoreType.DMA, pltpu.SemaphoreType.DMA],
    )
    return pl.pallas_call(
        _moe_kernel,
        grid_spec=grid_spec,
        out_shape=jax.ShapeDtypeStruct((n_out_rows, d), F32),
        compiler_params=_params(("arbitrary", "arbitrary")),
        name="moe",
    )(tile_expert, n_active, route, h, wg, wu, wd)


def _route_plan(r, n_tok, tm):
    n_assign = 2 * n_tok
    n_tiles = n_assign // tm + N_EXPERTS
    e_flat = jnp.concatenate([r[:, 0], r[:, 1]]).astype(jnp.int32)
    experts = jnp.arange(N_EXPERTS, dtype=jnp.int32)
    counts = jnp.sum(e_flat[:, None] == experts[None, :], axis=0).astype(jnp.int32)
    tiles_per = (counts + tm - 1) // tm
    tend = jnp.cumsum(tiles_per)
    n_active = tend[-1]
    j = jnp.arange(n_tiles, dtype=jnp.int32)
    te = jnp.minimum(jnp.sum(j[:, None] >= tend[None, :], axis=1), N_EXPERTS - 1).astype(jnp.int32)
    last = jnp.sum(jnp.where(j == n_active - 1, te, 0))
    te = jnp.where(j < n_active, te, last)
    pad_id = jnp.arange(tm, dtype=jnp.int32)
    pad_on = pad_id[None, :] < (tiles_per * tm - counts)[:, None]
    pad_key = jnp.where(pad_on, 2 * experts[:, None] + 1, 2 * N_EXPERTS).reshape(-1)
    keys = jnp.concatenate([2 * e_flat, pad_key])
    item = jnp.argsort(keys, stable=True).astype(jnp.int32)
    row = jnp.arange(n_tiles * tm, dtype=jnp.int32)
    real = item < n_assign
    src = jnp.where(real, item % n_tok, 0)
    dst = jnp.where(real, item, n_assign + row % tm)
    body = src | (dst << SRC_BITS)
    edge = (jnp.arange(tm, dtype=jnp.int32) + n_assign) << SRC_BITS
    route = jnp.concatenate([edge, body, edge]).astype(jnp.int32)
    return te, n_active.reshape(1).astype(jnp.int32), route


def _combine_kernel(x_ref, y0_ref, y1_ref, r_ref, mod_ref, g_ref, o_ref):
    r = r_ref[...]
    y = r[:, 2:3] * y0_ref[...] + r[:, 3:4] * y1_ref[...]
    x = x_ref[...] + mod_ref[0, 5:6, :] * y
    ms = jnp.mean(x * x, axis=-1, keepdims=True)
    o_ref[...] = x * lax.rsqrt(ms + EPS) * g_ref[...]


def _combine(xs, y, r, mods, mod_index, g, tm=MM_TILE):
    t, d = xs.shape
    nb = t // tm
    return pl.pallas_call(
        _combine_kernel,
        grid=(nb,),
        in_specs=[pl.BlockSpec((tm, d), lambda i: (i, 0)),
                  pl.BlockSpec((tm, d), lambda i: (i, 0)),
                  pl.BlockSpec((tm, d), lambda i: (nb + i, 0)),
                  pl.BlockSpec((tm, LANES), lambda i: (i, 0)),
                  pl.BlockSpec((1, N_MOD, d), mod_index),
                  _resident((1, d))],
        out_specs=pl.BlockSpec((tm, d), lambda i: (i, 0)),
        out_shape=jax.ShapeDtypeStruct((t, d), F32),
        compiler_params=_params(("arbitrary",)),
        name="combine",
    )(xs, y, y, r, mods, g.reshape(1, d))


def _rope_tables(seq, extra_rows):
    rows = seq // GRID_W
    row = jnp.repeat(jnp.arange(rows), GRID_W).astype(F32)
    col = jnp.tile(jnp.arange(GRID_W), rows).astype(F32)
    inv = ROPE_THETA ** (-jnp.arange(ROPE_FREQS, dtype=F32) / ROPE_FREQS)
    ar = row[:, None] * inv
    ac = col[:, None] * inv
    cos = jnp.concatenate([jnp.cos(ar), jnp.cos(ar), jnp.cos(ac), jnp.cos(ac)], axis=1)
    sin = jnp.concatenate([-jnp.sin(ar), jnp.sin(ar), -jnp.sin(ac), jnp.sin(ac)], axis=1)
    pad = LANES - QK_ROPE
    cos = jnp.concatenate([cos, jnp.ones((seq, pad), F32)], axis=1)
    sin = jnp.concatenate([sin, jnp.zeros((seq, pad), F32)], axis=1)
    cos = jnp.concatenate([cos, jnp.ones((extra_rows, LANES), F32)], axis=0)
    sin = jnp.concatenate([sin, jnp.zeros((extra_rows, LANES), F32)], axis=0)
    return cos, sin


def kernel(x, c, ctx, c_ctx, ada_w, ada_b, norm1_g, norm2_g, e_w_in, e_pool_w, e_pool_scale, e_conv_w, e_conv_b, e_conv_ln_g, e_conv_ln_b, e_w_out, e_ffn_w_gate, e_ffn_w_up, e_ffn_w_down, o_w_in, o_q_norm_g, o_w_qb, o_kv_norm_g, o_w_kvb, o_sgu_ln_g, o_sgu_ln_b, o_sgu_w, o_sgu_b, o_w_out, o_router_w, o_router_b, o_exp_w_gate, o_exp_w_up, o_exp_w_down, final_norm_g):
    n_batch, seq, d = x.shape
    ctx_len = ctx.shape[1]
    n_lat = n_batch * seq
    n_ctx = n_batch * ctx_len
    assert ctx_len % ROW_TILE == 0 and seq % MM_TILE == 0 and n_ctx % MM_TILE == 0
    assert ada_w.shape[0] == 2 and e_w_in.shape[0] == 1 and o_w_in.shape[0] == 1

    rows_per_layer = 8 * ((n_batch + 1 + 7) // 8)
    cond = jnp.zeros((rows_per_layer, d), F32).at[:n_batch].set(c).at[n_batch].set(c_ctx)
    mods = _ada(cond, ada_w, ada_b).reshape(2 * rows_per_layer, N_MOD, d)

    def mod_index(layer, tm):
        return _mod_spec(layer, tm, n_lat, seq, n_batch, rows_per_layer)

    x_lat = x.reshape(n_lat, d)
    x_ctx = ctx.reshape(n_ctx, d)

    p = _inproj(x_lat, x_ctx, mods, mod_index(0, MM_TILE), norm1_g[0], e_w_in[0].astype(BF16))
    y = _poolconv(p, e_pool_w[0].astype(BF16), e_pool_scale[0], e_conv_w[0], e_conv_b[0],
                  e_conv_ln_g[0], e_conv_ln_b[0], n_lat, seq, ctx_len)
    x1, h2 = _outproj([y], e_w_out[0].astype(BF16), (x_lat, x_ctx), mods, mod_index(0, MM_TILE), norm2_g[0],
                      n_lat + n_ctx, BF16)
    x2 = _ffn(h2, e_ffn_w_gate[0].astype(BF16), e_ffn_w_up[0].astype(BF16),
              e_ffn_w_down[0].astype(BF16), x1, mods, mod_index(0, MM_TILE))

    w_in = o_w_in[0]
    c_kr = Q_LORA + KV_LORA
    c_u = c_kr + QK_ROPE
    w_in_r = jnp.concatenate(
        [w_in[:, :c_kr], w_in[:, c_u:], w_in[:, c_kr:c_u], jnp.zeros((d, LANES - QK_ROPE), F32)],
        axis=1).astype(BF16)
    wq = o_w_qb[0].reshape(Q_LORA, MLA_HEADS, QK_NOPE + QK_ROPE)
    wq = jnp.pad(wq, ((0, 0), (0, 0), (0, HEAD_PAD - QK_NOPE - QK_ROPE)))
    wq = wq.reshape(Q_LORA, MLA_HEADS * HEAD_PAD).astype(BF16)
    wkv = o_w_kvb[0].reshape(KV_LORA, MLA_HEADS, QK_NOPE + V_HEAD)
    wkv = jnp.concatenate([wkv[:, :, :QK_NOPE].reshape(KV_LORA, -1),
                           wkv[:, :, QK_NOPE:].reshape(KV_LORA, -1)], axis=1).astype(BF16)
    cos_t, sin_t = _rope_tables(seq, ROW_TILE)
    d_sgu = o_sgu_ln_g.shape[1]
    sgu_bias = jnp.repeat(o_sgu_b[0].T, d_sgu // SGU_HEADS, axis=1)
    q, k, v, sg = _mla_proj(x2, mods, mod_index(1, ROW_TILE), norm1_g[1], w_in_r,
                            o_q_norm_g[0], wq, o_kv_norm_g[0], wkv, cos_t, sin_t,
                            o_sgu_ln_g[0], o_sgu_ln_b[0], o_sgu_w[0].astype(BF16), sgu_bias,
                            n_lat, seq)
    attn = _attention(q, k, v, n_batch, seq, ctx_len)

    rw = jnp.pad(o_router_w[0], ((0, 0), (0, LANES - N_EXPERTS))).astype(BF16)
    rb = jnp.concatenate([o_router_b[0], jnp.full((LANES - N_EXPERTS,), NEG, F32)]).reshape(1, LANES)
    x3, h3, r = _outproj([attn, sg], o_w_out[0].astype(BF16), (x2,), mods, mod_index(1, MM_TILE),
                         norm2_g[1], n_lat, F32, router=(rw, rb))

    te, n_active, route = _route_plan(r, n_lat, MOE_TM)
    ys = _moe(h3, o_exp_w_gate[0].astype(BF16), o_exp_w_up[0].astype(BF16),
              o_exp_w_down[0].astype(BF16), te, n_active, route, 2 * n_lat + MOE_TM)
    out = _combine(x3, ys, r, mods, mod_index(1, MM_TILE), final_norm_g)
    return out.reshape(n_batch, seq, d)
```

```python
import functools

import jax
import jax.numpy as jnp
from jax import lax
from jax.experimental import pallas as pl
from jax.experimental.pallas import tpu as pltpu

F32 = jnp.float32
BF16 = jnp.bfloat16

EPS = 1e-6
N_MOD = 6
GRID_W = 64
POOL_WINDOWS = (2, 4, 8, 16)
CONV_WIDTH = 31
MLA_HEADS = 8
Q_LORA = 512
KV_LORA = 512
QK_NOPE = 128
QK_ROPE = 64
V_HEAD = 128
ROPE_FREQS = QK_ROPE // 4
ROPE_THETA = 10000.0
SM_SCALE = (QK_NOPE + QK_ROPE) ** -0.5
SGU_HEADS = 8
CHUNK = 128
N_EXPERTS = 8

LANES = 128
SUBLANES = 8
HALO = 16
VMEM_LIMIT = 56 * 1024 * 1024
HEAD_PAD = 256
ROW_TILE = 256
MM_TILE = 512
FFN_TF = 512
MOE_TM = 512
MOE_NF = 7
ATTN_HEADS_PER_STEP = 2
NEG = -1e30


def _params(sem):
    return pltpu.CompilerParams(dimension_semantics=sem, vmem_limit_bytes=VMEM_LIMIT)


def _resident(shape):
    nd = len(shape)
    return pl.BlockSpec(shape, lambda *_: (0,) * nd, pipeline_mode=pl.Buffered(1))


def _rmsnorm_mod(x, g, shift, scale):
    ms = jnp.mean(x * x, axis=-1, keepdims=True)
    return (x * lax.rsqrt(ms + EPS) * g) * (1.0 + scale) + shift


def _layernorm(x, g, b):
    mu = jnp.mean(x, axis=-1, keepdims=True)
    xc = x - mu
    var = jnp.mean(xc * xc, axis=-1, keepdims=True)
    return xc * lax.rsqrt(var + EPS) * g + b


def _silu(x):
    return x * jax.nn.sigmoid(x)


def _ada_kernel(c_ref, w_ref, b_ref, o_ref):
    s = _silu(c_ref[...]).astype(BF16)
    o_ref[0] = jnp.dot(s, w_ref[0].astype(BF16), preferred_element_type=F32) + b_ref[0]


def _ada(cond, ada_w, ada_b, tn=1024):
    depth, d, n = ada_w.shape
    rows = cond.shape[0]
    return pl.pallas_call(
        _ada_kernel,
        grid=(depth, n // tn),
        in_specs=[pl.BlockSpec((rows, d), lambda l, j: (0, 0)),
                  pl.BlockSpec((1, d, tn), lambda l, j: (l, 0, j)),
                  pl.BlockSpec((1, 1, tn), lambda l, j: (l, 0, j))],
        out_specs=pl.BlockSpec((1, rows, tn), lambda l, j: (l, 0, j)),
        out_shape=jax.ShapeDtypeStruct((depth, rows, n), F32),
        compiler_params=_params(("arbitrary", "arbitrary")),
        name="ada",
    )(cond, ada_w, ada_b.reshape(depth, 1, n))


def _mod_spec(layer, tm, n_lat_rows, seq, n_batch, rows_per_layer):
    def index(i, *_):
        r = jnp.where(i * tm < n_lat_rows, (i * tm) // seq, n_batch)
        return (layer * rows_per_layer + r, 0, 0)
    return index


def _pick_rows(i, n_lat_tiles, lat_ref, ctx_ref):
    return jnp.where(i < n_lat_tiles, lat_ref[...], ctx_ref[...])


def _two_source_specs(tm, d, n_lat_tiles):
    return [pl.BlockSpec((tm, d), lambda i: (jnp.minimum(i, n_lat_tiles - 1), 0)),
            pl.BlockSpec((tm, d), lambda i: (jnp.maximum(i - n_lat_tiles, 0), 0))]


def _inproj_kernel(xl_ref, xc_ref, mod_ref, g_ref, w_ref, o_ref, *, n_lat_tiles):
    x = _pick_rows(pl.program_id(0), n_lat_tiles, xl_ref, xc_ref)
    h = _rmsnorm_mod(x, g_ref[...], mod_ref[0, 0:1, :], mod_ref[0, 1:2, :])
    o_ref[...] = jnp.dot(h.astype(BF16), w_ref[...], preferred_element_type=F32).astype(o_ref.dtype)


def _inproj(x_lat, x_ctx, mods, mod_index, g, w, tm=MM_TILE):
    d = x_lat.shape[1]
    t = x_lat.shape[0] + x_ctx.shape[0]
    n = w.shape[1]
    n_lat_tiles = x_lat.shape[0] // tm
    return pl.pallas_call(
        functools.partial(_inproj_kernel, n_lat_tiles=n_lat_tiles),
        grid=(t // tm,),
        in_specs=_two_source_specs(tm, d, n_lat_tiles) + [
            pl.BlockSpec((1, N_MOD, d), mod_index), _resident((1, d)), _resident((d, n))],
        out_specs=pl.BlockSpec((tm, n), lambda i: (i, 0)),
        out_shape=jax.ShapeDtypeStruct((t, n), BF16),
        compiler_params=_params(("arbitrary",)),
        name="inproj0",
    )(x_lat, x_ctx, mods, g.reshape(1, d), w)


def _poolconv_kernel(up_ref, um_ref, un_ref, ap_ref, am_ref, an_ref, gp_ref, gm_ref, gn_ref,
                     pw_ref, ps_ref, cw_ref, cb_ref, lg_ref, lb_ref, y_ref,
                     ubuf, zbuf, zsh, cbuf, *, n_lat_tiles, seq, ctx_len):
    i = pl.program_id(0)
    tm = um_ref.shape[0]
    dp = um_ref.shape[1]
    is_lat = i < n_lat_tiles
    pos0 = jnp.where(is_lat, (i * tm) % seq, ((i - n_lat_tiles) * tm) % ctx_len)
    length = jnp.where(is_lat, seq, ctx_len)
    keep_p = jnp.where(pos0 == 0, 0.0, 1.0).astype(F32)
    keep_n = jnp.where(pos0 + tm == length, 0.0, 1.0).astype(F32)

    def glu(a_ref, g_ref):
        return a_ref[...].astype(F32) * jax.nn.sigmoid(g_ref[...].astype(F32))

    ubuf[0:HALO, :] = up_ref[...].astype(F32) * keep_p
    ubuf[HALO:HALO + tm, :] = um_ref[...].astype(F32)
    ubuf[HALO + tm:, :] = un_ref[...].astype(F32) * keep_n
    zbuf[0:HALO, :] = glu(ap_ref, gp_ref) * keep_p
    zbuf[HALO:HALO + tm, :] = glu(am_ref, gm_ref)
    zbuf[HALO + tm:, :] = glu(an_ref, gn_ref) * keep_n

    pos = pos0 + lax.broadcasted_iota(jnp.int32, (tm, 1), 0)
    gdim = dp // len(POOL_WINDOWS)
    for g, w in enumerate(POOL_WINDOWS):
        cols = slice(g * gdim, (g + 1) * gdim)
        s = ubuf[HALO - w // 2:HALO - w // 2 + tm, cols]
        for o in range(-w // 2 + 1, w // 2):
            s = s + ubuf[HALO + o:HALO + o + tm, cols]
        cnt = jnp.minimum(pos - w // 2 + w, length) - jnp.maximum(pos - w // 2, 0)
        pooled = s / cnt.astype(F32) - ubuf[HALO:HALO + tm, cols]
        mixed = jnp.dot(pooled.astype(BF16), pw_ref[g], preferred_element_type=F32)
        y_ref[:, cols] = (mixed * ps_ref[:, cols]).astype(y_ref.dtype)

    n_sh = zsh.shape[1]
    for j in range(1, SUBLANES):
        zsh[j - 1] = zbuf[j:j + n_sh, :]
    rb = 64
    first = HALO - CONV_WIDTH // 2
    for c in range(0, zbuf.shape[1], LANES):
        cols = slice(c, c + LANES)
        taps = [cw_ref[k:k + 1, cols] for k in range(CONV_WIDTH)]
        for r in range(0, tm, rb):
            acc = None
            for k in range(CONV_WIDTH):
                q, j = divmod(first + k, SUBLANES)
                rows = slice(q * SUBLANES + r, q * SUBLANES + r + rb)
                z = zbuf[rows, cols] if j == 0 else zsh[j - 1, rows, cols]
                acc = taps[k] * z if acc is None else acc + taps[k] * z
            cbuf[r:r + rb, cols] = acc
    conv = _layernorm(cbuf[...] + cb_ref[...], lg_ref[...], lb_ref[...])
    y_ref[:, dp:] = _silu(conv).astype(y_ref.dtype)


def _poolconv(p, pool_w, pool_scale, conv_w, conv_b, ln_g, ln_b, n_lat_rows, seq, ctx_len,
              tm=ROW_TILE):
    t = p.shape[0]
    dp = pool_scale.shape[0]
    dc = conv_b.shape[0]
    hb = tm // HALO
    last_hb = t // HALO - 1

    def main(c):
        return pl.BlockSpec((tm, dp), lambda i: (i, c))

    def prev(c):
        return pl.BlockSpec((HALO, dp), lambda i: (jnp.maximum(i * hb - 1, 0), c))

    def nxt(c):
        return pl.BlockSpec((HALO, dp), lambda i: (jnp.minimum((i + 1) * hb, last_hb), c))

    kern = functools.partial(_poolconv_kernel, n_lat_tiles=n_lat_rows // tm, seq=seq, ctx_len=ctx_len)
    return pl.pallas_call(
        kern,
        grid=(t // tm,),
        in_specs=[prev(0), main(0), nxt(0), prev(1), main(1), nxt(1), prev(2), main(2), nxt(2),
                  _resident(pool_w.shape), _resident((1, dp)), _resident(conv_w.shape),
                  _resident((1, dc)), _resident((1, dc)), _resident((1, dc))],
        out_specs=pl.BlockSpec((tm, dp + dc), lambda i: (i, 0)),
        out_shape=jax.ShapeDtypeStruct((t, dp + dc), BF16),
        scratch_shapes=[pltpu.VMEM((tm + 2 * HALO, dp), F32),
                        pltpu.VMEM((tm + 2 * HALO, dc), F32),
                        pltpu.VMEM((SUBLANES - 1, tm + 2 * HALO - SUBLANES, dc), F32),
                        pltpu.VMEM((tm, dc), F32)],
        compiler_params=_params(("arbitrary",)),
        name="poolconv",
    )(p, p, p, p, p, p, p, p, p, pool_w, pool_scale.reshape(1, dp), conv_w,
      conv_b.reshape(1, dc), ln_g.reshape(1, dc), ln_b.reshape(1, dc))


def _outproj_kernel(*refs, n_in, n_res, n_lat_tiles, router):
    y_refs = refs[:n_in]
    w_ref = refs[n_in]
    x_refs = refs[n_in + 1:n_in + 1 + n_res]
    mod_ref, g_ref = refs[n_in + 1 + n_res:n_in + 3 + n_res]
    rest = refs[n_in + 3 + n_res:]
    if router:
        rw_ref, rb_ref, xo_ref, ho_ref, r_ref = rest
    else:
        xo_ref, ho_ref = rest
    o = None
    k0 = 0
    for y_ref in y_refs:
        kk = y_ref.shape[1]
        part = jnp.dot(y_ref[...], w_ref[k0:k0 + kk, :], preferred_element_type=F32)
        o = part if o is None else o + part
        k0 += kk
    if n_res == 2:
        x = _pick_rows(pl.program_id(0), n_lat_tiles, x_refs[0], x_refs[1])
    else:
        x = x_refs[0][...]
    x1 = x + mod_ref[0, 2:3, :] * o
    xo_ref[...] = x1
    h = _rmsnorm_mod(x1, g_ref[...], mod_ref[0, 3:4, :], mod_ref[0, 4:5, :])
    ho_ref[...] = h.astype(ho_ref.dtype)
    if router:
        logits = jnp.dot(h.astype(BF16), rw_ref[...], preferred_element_type=F32) + rb_ref[...]
        lane = lax.broadcasted_iota(jnp.int32, logits.shape, 1)
        m1 = jnp.max(logits, axis=-1, keepdims=True)
        i1 = jnp.min(jnp.where(logits == m1, lane, LANES), axis=-1, keepdims=True)
        rest_l = jnp.where(lane == i1, NEG * 2, logits)
        m2 = jnp.max(rest_l, axis=-1, keepdims=True)
        i2 = jnp.min(jnp.where(rest_l == m2, lane, LANES), axis=-1, keepdims=True)
        e2 = jnp.exp(m2 - m1)
        p1 = 1.0 / (1.0 + e2)
        p2 = e2 / (1.0 + e2)
        r = jnp.where(lane == 0, i1.astype(F32), 0.0)
        r = jnp.where(lane == 1, i2.astype(F32), r)
        r = jnp.where(lane == 2, p1, r)
        r = jnp.where(lane == 3, p2, r)
        r_ref[...] = r


def _outproj(ys, w, res, mods, mod_index, g, n_rows, h_dtype, router=None, tm=MM_TILE):
    d = w.shape[1]
    in_specs = [pl.BlockSpec((tm, y.shape[1]), lambda i: (i, 0)) for y in ys]
    in_specs.append(_resident(w.shape))
    n_lat_tiles = res[0].shape[0] // tm
    if len(res) == 2:
        in_specs += _two_source_specs(tm, d, n_lat_tiles)
    else:
        in_specs.append(pl.BlockSpec((tm, d), lambda i: (i, 0)))
    in_specs += [pl.BlockSpec((1, N_MOD, d), mod_index), _resident((1, d))]
    args = list(ys) + [w] + list(res) + [mods, g.reshape(1, d)]
    out_specs = [pl.BlockSpec((tm, d), lambda i: (i, 0)), pl.BlockSpec((tm, d), lambda i: (i, 0))]
    out_shape = [jax.ShapeDtypeStruct((n_rows, d), F32), jax.ShapeDtypeStruct((n_rows, d), h_dtype)]
    if router is not None:
        rw, rb = router
        in_specs += [_resident(rw.shape), _resident(rb.shape)]
        args += [rw, rb]
        out_specs.append(pl.BlockSpec((tm, LANES), lambda i: (i, 0)))
        out_shape.append(jax.ShapeDtypeStruct((n_rows, LANES), F32))
    kern = functools.partial(_outproj_kernel, n_in=len(ys), n_res=len(res), n_lat_tiles=n_lat_tiles,
                             router=router is not None)
    return pl.pallas_call(
        kern,
        grid=(n_rows // tm,),
        in_specs=in_specs,
        out_specs=out_specs,
        out_shape=out_shape,
        compiler_params=_params(("arbitrary",)),
        name="outproj_router" if router is not None else "outproj",
    )(*args)


def _ffn_kernel(h_ref, wg_ref, wu_ref, wd_ref, x_ref, mod_ref, o_ref, acc_ref):
    f = pl.program_id(1)

    @pl.when(f == 0)
    def _():
        acc_ref[...] = jnp.zeros(acc_ref.shape, F32)

    h = h_ref[...]
    a = jnp.dot(h, wg_ref[...], preferred_element_type=F32)
    b = jnp.dot(h, wu_ref[...], preferred_element_type=F32)
    acc_ref[...] += jnp.dot((_silu(a) * b).astype(BF16), wd_ref[...], preferred_element_type=F32)

    @pl.when(f == pl.num_programs(1) - 1)
    def _():
        o_ref[...] = x_ref[...] + mod_ref[0, 5:6, :] * acc_ref[...]


def _ffn(h, wg, wu, wd, xs, mods, mod_index, tm=MM_TILE, tf=FFN_TF):
    t, d = xs.shape
    ff = wg.shape[1]
    return pl.pallas_call(
        _ffn_kernel,
        grid=(t // tm, ff // tf),
        in_specs=[pl.BlockSpec((tm, d), lambda i, f: (i, 0)),
                  pl.BlockSpec((d, tf), lambda i, f: (0, f)),
                  pl.BlockSpec((d, tf), lambda i, f: (0, f)),
                  pl.BlockSpec((tf, d), lambda i, f: (f, 0)),
                  pl.BlockSpec((tm, d), lambda i, f: (i, 0)),
                  pl.BlockSpec((1, N_MOD, d), mod_index)],
        out_specs=pl.BlockSpec((tm, d), lambda i, f: (i, 0)),
        out_shape=jax.ShapeDtypeStruct((t, d), F32),
        scratch_shapes=[pltpu.VMEM((tm, d), F32)],
        compiler_params=_params(("arbitrary", "arbitrary")),
        name="ffn",
    )(h, wg, wu, wd, xs, mods)


def _rope(x, c, s):
    lane = lax.broadcasted_iota(jnp.int32, x.shape, 1)
    partner = jnp.where(lane % 32 < 16, pltpu.roll(x, LANES - 16, 1), pltpu.roll(x, 16, 1))
    return x * c + partner * s


def _mla_kernel(x_ref, mod_ref, g_ref, win_ref, qg_ref, wq_ref, kvg_ref, wkv_ref,
                cos_ref, sin_ref, slg_ref, slb_ref, sw_ref, sb_ref,
                q_ref, k_ref, v_ref, sg_ref, *, n_lat_tiles):
    i = pl.program_id(0)
    tm = x_ref.shape[0]
    h = _rmsnorm_mod(x_ref[...], g_ref[...], mod_ref[0, 0:1, :], mod_ref[0, 1:2, :]).astype(BF16)
    p = jnp.dot(h, win_ref[...], preferred_element_type=F32)
    c0 = Q_LORA
    c1 = c0 + KV_LORA
    d_sgu = slg_ref.shape[1]
    c2 = c1 + d_sgu
    c3 = c2 + d_sgu
    cos = cos_ref[...]
    sin = sin_ref[...]

    def rms(x, g):
        return x * lax.rsqrt(jnp.mean(x * x, axis=-1, keepdims=True) + EPS) * g

    ckv = rms(p[:, c0:c1], kvg_ref[...]).astype(BF16)
    kv = jnp.dot(ckv, wkv_ref[...], preferred_element_type=F32)
    kr = _rope(p[:, c3:c3 + LANES], cos, sin).astype(BF16)
    nk = MLA_HEADS * QK_NOPE
    for hd in range(MLA_HEADS):
        k_ref[:, hd * HEAD_PAD:hd * HEAD_PAD + QK_NOPE] = kv[:, hd * QK_NOPE:(hd + 1) * QK_NOPE].astype(BF16)
        k_ref[:, hd * HEAD_PAD + QK_NOPE:(hd + 1) * HEAD_PAD] = kr
    v_ref[...] = kv[:, nk:].astype(BF16)

    @pl.when(i < n_lat_tiles)
    def _():
        cq = rms(p[:, :c0], qg_ref[...]).astype(BF16)
        q = jnp.dot(cq, wq_ref[...], preferred_element_type=F32) * SM_SCALE
        for hd in range(MLA_HEADS):
            b0 = hd * HEAD_PAD
            q_ref[:, b0:b0 + QK_NOPE] = q[:, b0:b0 + QK_NOPE].astype(BF16)
            q_ref[:, b0 + QK_NOPE:b0 + HEAD_PAD] = _rope(q[:, b0 + QK_NOPE:b0 + HEAD_PAD], cos, sin).astype(BF16)

        zu = jax.nn.gelu(p[:, c1:c2])
        zv = _layernorm(jax.nn.gelu(p[:, c2:c3]), slg_ref[...], slb_ref[...]).astype(BF16)
        hdim = d_sgu // SGU_HEADS
        for ck in range(tm // CHUNK):
            rows = slice(ck * CHUNK, (ck + 1) * CHUNK)
            for hd in range(SGU_HEADS):
                cols = slice(hd * hdim, (hd + 1) * hdim)
                mixed = jnp.dot(sw_ref[hd], zv[rows, cols], preferred_element_type=F32) + sb_ref[:, cols]
                sg_ref[rows, cols] = (zu[rows, cols] * mixed).astype(BF16)


def _mla_proj(xs, mods, mod_index, g, w_in, qg, wq, kvg, wkv, cos_t, sin_t, slg, slb, sw, sb,
              n_lat_rows, seq, tm=ROW_TILE):
    t, d = xs.shape
    n_lat_tiles = n_lat_rows // tm
    seq_tiles = seq // tm
    d_sgu = slg.shape[0]

    def lat(i):
        return (jnp.minimum(i, n_lat_tiles - 1), 0)

    def table(i):
        return (jnp.where(i < n_lat_tiles, i % seq_tiles, seq_tiles), 0)

    kern = functools.partial(_mla_kernel, n_lat_tiles=n_lat_tiles)
    return pl.pallas_call(
        kern,
        grid=(t // tm,),
        in_specs=[pl.BlockSpec((tm, d), lambda i: (i, 0)),
                  pl.BlockSpec((1, N_MOD, d), mod_index),
                  _resident((1, d)), _resident(w_in.shape),
                  _resident((1, Q_LORA)), _resident(wq.shape),
                  _resident((1, KV_LORA)), _resident(wkv.shape),
                  pl.BlockSpec((tm, LANES), table), pl.BlockSpec((tm, LANES), table),
                  _resident((1, d_sgu)), _resident((1, d_sgu)), _resident(sw.shape), _resident(sb.shape)],
        out_specs=[pl.BlockSpec((tm, MLA_HEADS * HEAD_PAD), lat),
                   pl.BlockSpec((tm, MLA_HEADS * HEAD_PAD), lambda i: (i, 0)),
                   pl.BlockSpec((tm, MLA_HEADS * V_HEAD), lambda i: (i, 0)),
                   pl.BlockSpec((tm, d_sgu), lat)],
        out_shape=[jax.ShapeDtypeStruct((n_lat_rows, MLA_HEADS * HEAD_PAD), BF16),
                   jax.ShapeDtypeStruct((t, MLA_HEADS * HEAD_PAD), BF16),
                   jax.ShapeDtypeStruct((t, MLA_HEADS * V_HEAD), BF16),
                   jax.ShapeDtypeStruct((n_lat_rows, d_sgu), BF16)],
        compiler_params=_params(("arbitrary",)),
        name="mla_proj",
    )(xs, mods, g.reshape(1, d), w_in, qg.reshape(1, -1), wq, kvg.reshape(1, -1), wkv,
      cos_t, sin_t, slg.reshape(1, -1), slb.reshape(1, -1), sw, sb)


def _attn_kernel(*refs, n_cast):
    q_ref, kc_ref, kl_ref, vc_ref, vl_ref = refs[:5]
    cast_in = refs[5:5 + n_cast]
    o_ref = refs[5 + n_cast]
    cast_out = refs[6 + n_cast:]
    dn = (((1,), (1,)), ((), ()))
    for hd in range(ATTN_HEADS_PER_STEP):
        qk = slice(hd * HEAD_PAD, (hd + 1) * HEAD_PAD)
        vv = slice(hd * V_HEAD, (hd + 1) * V_HEAD)
        q = q_ref[:, qk]
        sc = lax.dot_general(q, kc_ref[:, qk], dn, preferred_element_type=F32)
        sl = lax.dot_general(q, kl_ref[:, qk], dn, preferred_element_type=F32)
        m = jnp.maximum(jnp.max(sc, axis=-1, keepdims=True), jnp.max(sl, axis=-1, keepdims=True))
        pc = jnp.exp(sc - m)
        pq = jnp.exp(sl - m)
        den = jnp.sum(pc, axis=-1, keepdims=True) + jnp.sum(pq, axis=-1, keepdims=True)
        o = (jnp.dot(pc.astype(BF16), vc_ref[:, vv], preferred_element_type=F32)
             + jnp.dot(pq.astype(BF16), vl_ref[:, vv], preferred_element_type=F32))
        o_ref[:, vv] = (o / den).astype(o_ref.dtype)
    for w_ref, wb_ref in zip(cast_in, cast_out):
        wb_ref[...] = w_ref[...].astype(BF16)


def _attention(q, k, v, n_batch, seq, ctx_len, casts, tq=ROW_TILE):
    n_lat_rows = n_batch * seq
    qb = seq // tq
    ctx_blk0 = n_lat_rows // ctx_len
    hs = ATTN_HEADS_PER_STEP
    hp = MLA_HEADS // hs
    n_steps = n_batch * hp * qb

    cast_specs = []
    for w in casts:
        n_exp, rows, cols = w.shape
        per_expert = n_steps // n_exp
        assert n_steps % n_exp == 0 and rows % (per_expert * 16) == 0

        def slab(b, h, j, per_expert=per_expert):
            step = (b * hp + h) * qb + j
            return (step // per_expert, step % per_expert, 0)
        cast_specs.append(pl.BlockSpec((1, rows // per_expert, cols), slab))
    outs = pl.pallas_call(
        functools.partial(_attn_kernel, n_cast=len(casts)),
        grid=(n_batch, hp, qb),
        in_specs=[pl.BlockSpec((tq, hs * HEAD_PAD), lambda b, h, j: (b * qb + j, h)),
                  pl.BlockSpec((ctx_len, hs * HEAD_PAD), lambda b, h, j: (ctx_blk0 + b, h)),
                  pl.BlockSpec((seq, hs * HEAD_PAD), lambda b, h, j: (b, h)),
                  pl.BlockSpec((ctx_len, hs * V_HEAD), lambda b, h, j: (ctx_blk0 + b, h)),
                  pl.BlockSpec((seq, hs * V_HEAD), lambda b, h, j: (b, h))] + cast_specs,
        out_specs=[pl.BlockSpec((tq, hs * V_HEAD), lambda b, h, j: (b * qb + j, h))] + cast_specs,
        out_shape=[jax.ShapeDtypeStruct((n_lat_rows, MLA_HEADS * V_HEAD), BF16)]
        + [jax.ShapeDtypeStruct(w.shape, BF16) for w in casts],
        compiler_params=_params(("arbitrary", "arbitrary", "arbitrary")),
        name="attention",
    )(q, k, k, v, v, *casts)
    return outs[0], outs[1:]


SRC_BITS = 15


def _moe_kernel(te_ref, nact_ref, route_ref, h_hbm, wg_ref, wu_ref, wd_ref, y_hbm,
                xbuf, xb, acc, gsem, ssem):
    i = pl.program_id(0)
    f = pl.program_id(1)
    tm = xbuf.shape[0]
    chunk = tm // (MOE_NF + 1)
    nact = nact_ref[0]
    active = i < nact
    slot = i % 2
    dump0 = y_hbm.shape[0] - tm

    def gather_copy(tile, r):
        src = route_ref[(tile + 1) * tm + r] & ((1 << SRC_BITS) - 1)
        return pltpu.make_async_copy(h_hbm.at[pl.ds(src, 1)], xbuf.at[pl.ds(r, 1)], gsem)

    def scatter_copy(tile, r):
        dst = route_ref[(tile + 1) * tm + r] >> SRC_BITS
        return pltpu.make_async_copy(acc.at[tile % 2, pl.ds(r, 1)], y_hbm.at[pl.ds(dst, 1)], ssem)

    def wait_gather():
        pltpu.make_async_copy(h_hbm.at[pl.ds(0, tm)], xbuf, gsem).wait()

    def wait_scatter():
        pltpu.make_async_copy(acc.at[0], y_hbm.at[pl.ds(0, tm)], ssem).wait()

    def start_all(copy, tile):
        def body(r, c):
            copy(tile, r).start()
            return c
        lax.fori_loop(0, tm, body, 0)

    @pl.when((i == 0) & (f == 0))
    def _():
        acc[1] = jnp.zeros(acc.shape[1:], F32)
        fill = pltpu.make_async_copy(acc.at[1], y_hbm.at[pl.ds(dump0, tm)], ssem)
        fill.start()
        fill.wait()
        start_all(gather_copy, 0)

    @pl.when((f == 0) & (i <= nact))
    def _():
        wait_gather()

        @pl.when(i >= 1)
        def _():
            wait_scatter()

    def issue_chunk(c):
        for j in range(chunk):
            r = c * chunk + j
            gather_copy(i + 1, r).start()
            scatter_copy(i - 1, r).start()

    @pl.when((f == 0) & active)
    def _():
        xb[...] = xbuf[...].astype(BF16)
        acc[slot] = jnp.zeros(acc.shape[1:], F32)
        issue_chunk(0)

    @pl.when((f == 0) & (i == nact) & (i >= 1))
    def _():
        start_all(scatter_copy, i - 1)
        wait_scatter()

    @pl.when(active)
    def _():
        issue_chunk(f + 1)
        x = xb[...]
        a = jnp.dot(x, wg_ref[0], preferred_element_type=F32)
        b = jnp.dot(x, wu_ref[0], preferred_element_type=F32)
        acc[slot] += jnp.dot((_silu(a) * b).astype(BF16), wd_ref[0], preferred_element_type=F32)


def _moe(h, wg, wu, wd, tile_expert, n_active, route, n_out_rows, tm=MOE_TM):
    d = h.shape[1]
    ff = wg.shape[2]
    n_tiles = tile_expert.shape[0]
    nf = MOE_NF
    tf = ff // nf

    def fidx(i, f, nact):
        return jnp.where(i < nact[0], f, nf - 1)

    grid_spec = pltpu.PrefetchScalarGridSpec(
        num_scalar_prefetch=3,
        grid=(n_tiles, nf),
        in_specs=[pl.BlockSpec(memory_space=pl.ANY),
                  pl.BlockSpec((1, d, tf), lambda i, f, te, nact, route: (te[i], 0, fidx(i, f, nact))),
                  pl.BlockSpec((1, d, tf), lambda i, f, te, nact, route: (te[i], 0, fidx(i, f, nact))),
                  pl.BlockSpec((1, tf, d), lambda i, f, te, nact, route: (te[i], fidx(i, f, nact), 0))],
        out_specs=pl.BlockSpec(memory_space=pl.ANY),
        scratch_shapes=[pltpu.VMEM((tm, d), F32), pltpu.VMEM((tm, d), BF16), pltpu.VMEM((2, tm, d), F32),
                        pltpu.SemaphoreType.DMA, pltpu.SemaphoreType.DMA],
    )
    return pl.pallas_call(
        _moe_kernel,
        grid_spec=grid_spec,
        out_shape=jax.ShapeDtypeStruct((n_out_rows, d), F32),
        compiler_params=_params(("arbitrary", "arbitrary")),
        name="moe",
    )(tile_expert, n_active, route, h, wg, wu, wd)


def _route_plan(r, n_tok, tm):
    n_assign = 2 * n_tok
    n_tiles = n_assign // tm + N_EXPERTS
    e_flat = jnp.concatenate([r[:, 0], r[:, 1]]).astype(jnp.int32)
    experts = jnp.arange(N_EXPERTS, dtype=jnp.int32)
    counts = jnp.sum(e_flat[:, None] == experts[None, :], axis=0).astype(jnp.int32)
    tiles_per = (counts + tm - 1) // tm
    tend = jnp.cumsum(tiles_per)
    n_active = tend[-1]
    j = jnp.arange(n_tiles, dtype=jnp.int32)
    te = jnp.minimum(jnp.sum(j[:, None] >= tend[None, :], axis=1), N_EXPERTS - 1).astype(jnp.int32)
    last = jnp.sum(jnp.where(j == n_active - 1, te, 0))
    te = jnp.where(j < n_active, te, last)
    pad_id = jnp.arange(tm, dtype=jnp.int32)
    pad_on = pad_id[None, :] < (tiles_per * tm - counts)[:, None]
    pad_key = jnp.where(pad_on, 2 * experts[:, None] + 1, 2 * N_EXPERTS).reshape(-1)
    keys = jnp.concatenate([2 * e_flat, pad_key])
    item = jnp.argsort(keys, stable=True).astype(jnp.int32)
    row = jnp.arange(n_tiles * tm, dtype=jnp.int32)
    real = item < n_assign
    src = jnp.where(real, item % n_tok, 0)
    dst = jnp.where(real, item, n_assign + row % tm)
    body = src | (dst << SRC_BITS)
    edge = (jnp.arange(tm, dtype=jnp.int32) + n_assign) << SRC_BITS
    route = jnp.concatenate([edge, body, edge]).astype(jnp.int32)
    return te, n_active.reshape(1).astype(jnp.int32), route


def _combine_kernel(x_ref, y0_ref, y1_ref, r_ref, mod_ref, g_ref, o_ref):
    r = r_ref[...]
    y = r[:, 2:3] * y0_ref[...] + r[:, 3:4] * y1_ref[...]
    x = x_ref[...] + mod_ref[0, 5:6, :] * y
    ms = jnp.mean(x * x, axis=-1, keepdims=True)
    o_ref[...] = x * lax.rsqrt(ms + EPS) * g_ref[...]


def _combine(xs, y, r, mods, mod_index, g, tm=MM_TILE):
    t, d = xs.shape
    nb = t // tm
    return pl.pallas_call(
        _combine_kernel,
        grid=(nb,),
        in_specs=[pl.BlockSpec((tm, d), lambda i: (i, 0)),
                  pl.BlockSpec((tm, d), lambda i: (i, 0)),
                  pl.BlockSpec((tm, d), lambda i: (nb + i, 0)),
                  pl.BlockSpec((tm, LANES), lambda i: (i, 0)),
                  pl.BlockSpec((1, N_MOD, d), mod_index),
                  _resident((1, d))],
        out_specs=pl.BlockSpec((tm, d), lambda i: (i, 0)),
        out_shape=jax.ShapeDtypeStruct((t, d), F32),
        compiler_params=_params(("arbitrary",)),
        name="combine",
    )(xs, y, y, r, mods, g.reshape(1, d))


def _rope_tables(seq, extra_rows):
    rows = seq // GRID_W
    row = jnp.repeat(jnp.arange(rows), GRID_W).astype(F32)
    col = jnp.tile(jnp.arange(GRID_W), rows).astype(F32)
    inv = ROPE_THETA ** (-jnp.arange(ROPE_FREQS, dtype=F32) / ROPE_FREQS)
    ar = row[:, None] * inv
    ac = col[:, None] * inv
    cos = jnp.concatenate([jnp.cos(ar), jnp.cos(ar), jnp.cos(ac), jnp.cos(ac)], axis=1)
    sin = jnp.concatenate([-jnp.sin(ar), jnp.sin(ar), -jnp.sin(ac), jnp.sin(ac)], axis=1)
    pad = LANES - QK_ROPE
    cos = jnp.concatenate([cos, jnp.ones((seq, pad), F32)], axis=1)
    sin = jnp.concatenate([sin, jnp.zeros((seq, pad), F32)], axis=1)
    cos = jnp.concatenate([cos, jnp.ones((extra_rows, LANES), F32)], axis=0)
    sin = jnp.concatenate([sin, jnp.zeros((extra_rows, LANES), F32)], axis=0)
    return cos, sin


def kernel(x, c, ctx, c_ctx, ada_w, ada_b, norm1_g, norm2_g, e_w_in, e_pool_w, e_pool_scale, e_conv_w, e_conv_b, e_conv_ln_g, e_conv_ln_b, e_w_out, e_ffn_w_gate, e_ffn_w_up, e_ffn_w_down, o_w_in, o_q_norm_g, o_w_qb, o_kv_norm_g, o_w_kvb, o_sgu_ln_g, o_sgu_ln_b, o_sgu_w, o_sgu_b, o_w_out, o_router_w, o_router_b, o_exp_w_gate, o_exp_w_up, o_exp_w_down, final_norm_g):
    n_batch, seq, d = x.shape
    ctx_len = ctx.shape[1]
    n_lat = n_batch * seq
    n_ctx = n_batch * ctx_len
    assert ctx_len % ROW_TILE == 0 and seq % MM_TILE == 0 and n_ctx % MM_TILE == 0
    assert ada_w.shape[0] == 2 and e_w_in.shape[0] == 1 and o_w_in.shape[0] == 1

    rows_per_layer = 8 * ((n_batch + 1 + 7) // 8)
    cond = jnp.zeros((rows_per_layer, d), F32).at[:n_batch].set(c).at[n_batch].set(c_ctx)
    mods = _ada(cond, ada_w, ada_b).reshape(2 * rows_per_layer, N_MOD, d)

    def mod_index(layer, tm):
        return _mod_spec(layer, tm, n_lat, seq, n_batch, rows_per_layer)

    x_lat = x.reshape(n_lat, d)
    x_ctx = ctx.reshape(n_ctx, d)

    p = _inproj(x_lat, x_ctx, mods, mod_index(0, MM_TILE), norm1_g[0], e_w_in[0].astype(BF16))
    y = _poolconv(p, e_pool_w[0].astype(BF16), e_pool_scale[0], e_conv_w[0], e_conv_b[0],
                  e_conv_ln_g[0], e_conv_ln_b[0], n_lat, seq, ctx_len)
    x1, h2 = _outproj([y], e_w_out[0].astype(BF16), (x_lat, x_ctx), mods, mod_index(0, MM_TILE), norm2_g[0],
                      n_lat + n_ctx, BF16)
    x2 = _ffn(h2, e_ffn_w_gate[0].astype(BF16), e_ffn_w_up[0].astype(BF16),
              e_ffn_w_down[0].astype(BF16), x1, mods, mod_index(0, MM_TILE))

    w_in = o_w_in[0]
    c_kr = Q_LORA + KV_LORA
    c_u = c_kr + QK_ROPE
    w_in_r = jnp.concatenate(
        [w_in[:, :c_kr], w_in[:, c_u:], w_in[:, c_kr:c_u], jnp.zeros((d, LANES - QK_ROPE), F32)],
        axis=1).astype(BF16)
    wq = o_w_qb[0].reshape(Q_LORA, MLA_HEADS, QK_NOPE + QK_ROPE)
    wq = jnp.pad(wq, ((0, 0), (0, 0), (0, HEAD_PAD - QK_NOPE - QK_ROPE)))
    wq = wq.reshape(Q_LORA, MLA_HEADS * HEAD_PAD).astype(BF16)
    wkv = o_w_kvb[0].reshape(KV_LORA, MLA_HEADS, QK_NOPE + V_HEAD)
    wkv = jnp.concatenate([wkv[:, :, :QK_NOPE].reshape(KV_LORA, -1),
                           wkv[:, :, QK_NOPE:].reshape(KV_LORA, -1)], axis=1).astype(BF16)
    cos_t, sin_t = _rope_tables(seq, ROW_TILE)
    d_sgu = o_sgu_ln_g.shape[1]
    sgu_bias = jnp.repeat(o_sgu_b[0].T, d_sgu // SGU_HEADS, axis=1)
    q, k, v, sg = _mla_proj(x2, mods, mod_index(1, ROW_TILE), norm1_g[1], w_in_r,
                            o_q_norm_g[0], wq, o_kv_norm_g[0], wkv, cos_t, sin_t,
                            o_sgu_ln_g[0], o_sgu_ln_b[0], o_sgu_w[0].astype(BF16), sgu_bias,
                            n_lat, seq)
    experts = (o_exp_w_gate[0], o_exp_w_up[0], o_exp_w_down[0])
    attn, (wg, wu, wd) = _attention(q, k, v, n_batch, seq, ctx_len, experts)

    rw = jnp.pad(o_router_w[0], ((0, 0), (0, LANES - N_EXPERTS))).astype(BF16)
    rb = jnp.concatenate([o_router_b[0], jnp.full((LANES - N_EXPERTS,), NEG, F32)]).reshape(1, LANES)
    x3, h3, r = _outproj([attn, sg], o_w_out[0].astype(BF16), (x2,), mods, mod_index(1, MM_TILE),
                         norm2_g[1], n_lat, F32, router=(rw, rb))

    te, n_active, route = _route_plan(r, n_lat, MOE_TM)
    ys = _moe(h3, wg, wu, wd, te, n_active, route, 2 * n_lat + MOE_TM)
    out = _combine(x3, ys, r, mods, mod_index(1, MM_TILE), final_norm_g)
    return out.reshape(n_batch, seq, d)
```

```python
import functools

import jax
import jax.numpy as jnp
from jax import lax
from jax.experimental import pallas as pl
from jax.experimental.pallas import tpu as pltpu

F32 = jnp.float32
BF16 = jnp.bfloat16

EPS = 1e-6
N_MOD = 6
GRID_W = 64
POOL_WINDOWS = (2, 4, 8, 16)
CONV_WIDTH = 31
MLA_HEADS = 8
Q_LORA = 512
KV_LORA = 512
QK_NOPE = 128
QK_ROPE = 64
V_HEAD = 128
ROPE_FREQS = QK_ROPE // 4
ROPE_THETA = 10000.0
SM_SCALE = (QK_NOPE + QK_ROPE) ** -0.5
LOG2E = 1.4426950408889634
SGU_HEADS = 8
CHUNK = 128
N_EXPERTS = 8

LANES = 128
SUBLANES = 8
HALO = 16
VMEM_LIMIT = 56 * 1024 * 1024
HEAD_PAD = 256
ROW_TILE = 256
MM_TILE = 512
FFN_TF = 512
MOE_TM = 512
MOE_NF = 7
ATTN_HEADS_PER_STEP = 2
NEG = -1e30


def _params(sem):
    return pltpu.CompilerParams(dimension_semantics=sem, vmem_limit_bytes=VMEM_LIMIT)


def _resident(shape):
    nd = len(shape)
    return pl.BlockSpec(shape, lambda *_: (0,) * nd, pipeline_mode=pl.Buffered(1))


def _rmsnorm_mod(x, g, shift, scale):
    ms = jnp.mean(x * x, axis=-1, keepdims=True)
    return (x * lax.rsqrt(ms + EPS) * g) * (1.0 + scale) + shift


def _layernorm(x, g, b):
    mu = jnp.mean(x, axis=-1, keepdims=True)
    xc = x - mu
    var = jnp.mean(xc * xc, axis=-1, keepdims=True)
    return xc * lax.rsqrt(var + EPS) * g + b


def _silu(x):
    return x * jax.nn.sigmoid(x)


def _ada_kernel(c_ref, w_ref, b_ref, o_ref):
    s = _silu(c_ref[...]).astype(BF16)
    o_ref[0] = jnp.dot(s, w_ref[0].astype(BF16), preferred_element_type=F32) + b_ref[0]


def _ada(cond, ada_w, ada_b, tn=1024):
    depth, d, n = ada_w.shape
    rows = cond.shape[0]
    return pl.pallas_call(
        _ada_kernel,
        grid=(depth, n // tn),
        in_specs=[pl.BlockSpec((rows, d), lambda l, j: (0, 0)),
                  pl.BlockSpec((1, d, tn), lambda l, j: (l, 0, j)),
                  pl.BlockSpec((1, 1, tn), lambda l, j: (l, 0, j))],
        out_specs=pl.BlockSpec((1, rows, tn), lambda l, j: (l, 0, j)),
        out_shape=jax.ShapeDtypeStruct((depth, rows, n), F32),
        compiler_params=_params(("arbitrary", "arbitrary")),
        name="ada",
    )(cond, ada_w, ada_b.reshape(depth, 1, n))


def _mod_spec(layer, tm, n_lat_rows, seq, n_batch, rows_per_layer):
    def index(i, *_):
        r = jnp.where(i * tm < n_lat_rows, (i * tm) // seq, n_batch)
        return (layer * rows_per_layer + r, 0, 0)
    return index


def _pick_rows(i, n_lat_tiles, lat_ref, ctx_ref):
    return jnp.where(i < n_lat_tiles, lat_ref[...], ctx_ref[...])


def _two_source_specs(tm, d, n_lat_tiles):
    return [pl.BlockSpec((tm, d), lambda i: (jnp.minimum(i, n_lat_tiles - 1), 0)),
            pl.BlockSpec((tm, d), lambda i: (jnp.maximum(i - n_lat_tiles, 0), 0))]


def _inproj_kernel(xl_ref, xc_ref, mod_ref, g_ref, w_ref, o_ref, *, n_lat_tiles):
    x = _pick_rows(pl.program_id(0), n_lat_tiles, xl_ref, xc_ref)
    h = _rmsnorm_mod(x, g_ref[...], mod_ref[0, 0:1, :], mod_ref[0, 1:2, :])
    o_ref[...] = jnp.dot(h.astype(BF16), w_ref[...], preferred_element_type=F32).astype(o_ref.dtype)


def _inproj(x_lat, x_ctx, mods, mod_index, g, w, tm=MM_TILE):
    d = x_lat.shape[1]
    t = x_lat.shape[0] + x_ctx.shape[0]
    n = w.shape[1]
    n_lat_tiles = x_lat.shape[0] // tm
    return pl.pallas_call(
        functools.partial(_inproj_kernel, n_lat_tiles=n_lat_tiles),
        grid=(t // tm,),
        in_specs=_two_source_specs(tm, d, n_lat_tiles) + [
            pl.BlockSpec((1, N_MOD, d), mod_index), _resident((1, d)), _resident((d, n))],
        out_specs=pl.BlockSpec((tm, n), lambda i: (i, 0)),
        out_shape=jax.ShapeDtypeStruct((t, n), BF16),
        compiler_params=_params(("arbitrary",)),
        name="inproj0",
    )(x_lat, x_ctx, mods, g.reshape(1, d), w)


def _poolconv_kernel(up_ref, um_ref, un_ref, ap_ref, am_ref, an_ref, gp_ref, gm_ref, gn_ref,
                     pw_ref, ps_ref, cw_ref, cb_ref, lg_ref, lb_ref, y_ref,
                     ubuf, zbuf, zsh, cbuf, *, n_lat_tiles, seq, ctx_len):
    i = pl.program_id(0)
    tm = um_ref.shape[0]
    dp = um_ref.shape[1]
    is_lat = i < n_lat_tiles
    pos0 = jnp.where(is_lat, (i * tm) % seq, ((i - n_lat_tiles) * tm) % ctx_len)
    length = jnp.where(is_lat, seq, ctx_len)
    keep_p = jnp.where(pos0 == 0, 0.0, 1.0).astype(F32)
    keep_n = jnp.where(pos0 + tm == length, 0.0, 1.0).astype(F32)

    def glu(a_ref, g_ref):
        return a_ref[...].astype(F32) * jax.nn.sigmoid(g_ref[...].astype(F32))

    ubuf[0:HALO, :] = up_ref[...].astype(F32) * keep_p
    ubuf[HALO:HALO + tm, :] = um_ref[...].astype(F32)
    ubuf[HALO + tm:, :] = un_ref[...].astype(F32) * keep_n
    zbuf[0:HALO, :] = glu(ap_ref, gp_ref) * keep_p
    zbuf[HALO:HALO + tm, :] = glu(am_ref, gm_ref)
    zbuf[HALO + tm:, :] = glu(an_ref, gn_ref) * keep_n

    pos = pos0 + lax.broadcasted_iota(jnp.int32, (tm, 1), 0)
    gdim = dp // len(POOL_WINDOWS)
    for g, w in enumerate(POOL_WINDOWS):
        cols = slice(g * gdim, (g + 1) * gdim)
        s = ubuf[HALO - w // 2:HALO - w // 2 + tm, cols]
        for o in range(-w // 2 + 1, w // 2):
            s = s + ubuf[HALO + o:HALO + o + tm, cols]
        cnt = jnp.minimum(pos - w // 2 + w, length) - jnp.maximum(pos - w // 2, 0)
        pooled = s / cnt.astype(F32) - ubuf[HALO:HALO + tm, cols]
        mixed = jnp.dot(pooled.astype(BF16), pw_ref[g], preferred_element_type=F32)
        y_ref[:, cols] = (mixed * ps_ref[:, cols]).astype(y_ref.dtype)

    n_sh = zsh.shape[1]
    for j in range(1, SUBLANES):
        zsh[j - 1] = zbuf[j:j + n_sh, :]
    rb = 64
    first = HALO - CONV_WIDTH // 2
    for c in range(0, zbuf.shape[1], LANES):
        cols = slice(c, c + LANES)
        taps = [cw_ref[k:k + 1, cols] for k in range(CONV_WIDTH)]
        for r in range(0, tm, rb):
            acc = None
            for k in range(CONV_WIDTH):
                q, j = divmod(first + k, SUBLANES)
                rows = slice(q * SUBLANES + r, q * SUBLANES + r + rb)
                z = zbuf[rows, cols] if j == 0 else zsh[j - 1, rows, cols]
                acc = taps[k] * z if acc is None else acc + taps[k] * z
            cbuf[r:r + rb, cols] = acc
    conv = _layernorm(cbuf[...] + cb_ref[...], lg_ref[...], lb_ref[...])
    y_ref[:, dp:] = _silu(conv).astype(y_ref.dtype)


def _poolconv(p, pool_w, pool_scale, conv_w, conv_b, ln_g, ln_b, n_lat_rows, seq, ctx_len,
              tm=ROW_TILE):
    t = p.shape[0]
    dp = pool_scale.shape[0]
    dc = conv_b.shape[0]
    hb = tm // HALO
    last_hb = t // HALO - 1

    def main(c):
        return pl.BlockSpec((tm, dp), lambda i: (i, c))

    def prev(c):
        return pl.BlockSpec((HALO, dp), lambda i: (jnp.maximum(i * hb - 1, 0), c))

    def nxt(c):
        return pl.BlockSpec((HALO, dp), lambda i: (jnp.minimum((i + 1) * hb, last_hb), c))

    kern = functools.partial(_poolconv_kernel, n_lat_tiles=n_lat_rows // tm, seq=seq, ctx_len=ctx_len)
    return pl.pallas_call(
        kern,
        grid=(t // tm,),
        in_specs=[prev(0), main(0), nxt(0), prev(1), main(1), nxt(1), prev(2), main(2), nxt(2),
                  _resident(pool_w.shape), _resident((1, dp)), _resident(conv_w.shape),
                  _resident((1, dc)), _resident((1, dc)), _resident((1, dc))],
        out_specs=pl.BlockSpec((tm, dp + dc), lambda i: (i, 0)),
        out_shape=jax.ShapeDtypeStruct((t, dp + dc), BF16),
        scratch_shapes=[pltpu.VMEM((tm + 2 * HALO, dp), F32),
                        pltpu.VMEM((tm + 2 * HALO, dc), F32),
                        pltpu.VMEM((SUBLANES - 1, tm + 2 * HALO - SUBLANES, dc), F32),
                        pltpu.VMEM((tm, dc), F32)],
        compiler_params=_params(("arbitrary",)),
        name="poolconv",
    )(p, p, p, p, p, p, p, p, p, pool_w, pool_scale.reshape(1, dp), conv_w,
      conv_b.reshape(1, dc), ln_g.reshape(1, dc), ln_b.reshape(1, dc))


def _outproj_kernel(*refs, n_in, n_res, n_lat_tiles, router):
    y_refs = refs[:n_in]
    w_ref = refs[n_in]
    x_refs = refs[n_in + 1:n_in + 1 + n_res]
    mod_ref, g_ref = refs[n_in + 1 + n_res:n_in + 3 + n_res]
    rest = refs[n_in + 3 + n_res:]
    if router:
        rw_ref, rb_ref, xo_ref, ho_ref, r_ref = rest
    else:
        xo_ref, ho_ref = rest
    o = None
    k0 = 0
    for y_ref in y_refs:
        kk = y_ref.shape[1]
        part = jnp.dot(y_ref[...], w_ref[k0:k0 + kk, :], preferred_element_type=F32)
        o = part if o is None else o + part
        k0 += kk
    if n_res == 2:
        x = _pick_rows(pl.program_id(0), n_lat_tiles, x_refs[0], x_refs[1])
    else:
        x = x_refs[0][...]
    x1 = x + mod_ref[0, 2:3, :] * o
    xo_ref[...] = x1
    h = _rmsnorm_mod(x1, g_ref[...], mod_ref[0, 3:4, :], mod_ref[0, 4:5, :])
    ho_ref[...] = h.astype(ho_ref.dtype)
    if router:
        logits = jnp.dot(h.astype(BF16), rw_ref[...], preferred_element_type=F32) + rb_ref[...]
        lane = lax.broadcasted_iota(jnp.int32, logits.shape, 1)
        m1 = jnp.max(logits, axis=-1, keepdims=True)
        i1 = jnp.min(jnp.where(logits == m1, lane, LANES), axis=-1, keepdims=True)
        rest_l = jnp.where(lane == i1, NEG * 2, logits)
        m2 = jnp.max(rest_l, axis=-1, keepdims=True)
        i2 = jnp.min(jnp.where(rest_l == m2, lane, LANES), axis=-1, keepdims=True)
        e2 = jnp.exp(m2 - m1)
        p1 = 1.0 / (1.0 + e2)
        p2 = e2 / (1.0 + e2)
        r = jnp.where(lane == 0, i1.astype(F32), 0.0)
        r = jnp.where(lane == 1, i2.astype(F32), r)
        r = jnp.where(lane == 2, p1, r)
        r = jnp.where(lane == 3, p2, r)
        r_ref[...] = r


def _outproj(ys, w, res, mods, mod_index, g, n_rows, h_dtype, router=None, tm=MM_TILE):
    d = w.shape[1]
    in_specs = [pl.BlockSpec((tm, y.shape[1]), lambda i: (i, 0)) for y in ys]
    in_specs.append(_resident(w.shape))
    n_lat_tiles = res[0].shape[0] // tm
    if len(res) == 2:
        in_specs += _two_source_specs(tm, d, n_lat_tiles)
    else:
        in_specs.append(pl.BlockSpec((tm, d), lambda i: (i, 0)))
    in_specs += [pl.BlockSpec((1, N_MOD, d), mod_index), _resident((1, d))]
    args = list(ys) + [w] + list(res) + [mods, g.reshape(1, d)]
    out_specs = [pl.BlockSpec((tm, d), lambda i: (i, 0)), pl.BlockSpec((tm, d), lambda i: (i, 0))]
    out_shape = [jax.ShapeDtypeStruct((n_rows, d), F32), jax.ShapeDtypeStruct((n_rows, d), h_dtype)]
    if router is not None:
        rw, rb = router
        in_specs += [_resident(rw.shape), _resident(rb.shape)]
        args += [rw, rb]
        out_specs.append(pl.BlockSpec((tm, LANES), lambda i: (i, 0)))
        out_shape.append(jax.ShapeDtypeStruct((n_rows, LANES), F32))
    kern = functools.partial(_outproj_kernel, n_in=len(ys), n_res=len(res), n_lat_tiles=n_lat_tiles,
                             router=router is not None)
    return pl.pallas_call(
        kern,
        grid=(n_rows // tm,),
        in_specs=in_specs,
        out_specs=out_specs,
        out_shape=out_shape,
        compiler_params=_params(("arbitrary",)),
        name="outproj_router" if router is not None else "outproj",
    )(*args)


def _ffn_kernel(h_ref, wg_ref, wu_ref, wd_ref, x_ref, mod_ref, o_ref, acc_ref):
    f = pl.program_id(1)

    @pl.when(f == 0)
    def _():
        acc_ref[...] = jnp.zeros(acc_ref.shape, F32)

    h = h_ref[...]
    a = jnp.dot(h, wg_ref[...], preferred_element_type=F32)
    b = jnp.dot(h, wu_ref[...], preferred_element_type=F32)
    acc_ref[...] += jnp.dot((_silu(a) * b).astype(BF16), wd_ref[...], preferred_element_type=F32)

    @pl.when(f == pl.num_programs(1) - 1)
    def _():
        o_ref[...] = x_ref[...] + mod_ref[0, 5:6, :] * acc_ref[...]


def _ffn(h, wg, wu, wd, xs, mods, mod_index, tm=MM_TILE, tf=FFN_TF):
    t, d = xs.shape
    ff = wg.shape[1]
    return pl.pallas_call(
        _ffn_kernel,
        grid=(t // tm, ff // tf),
        in_specs=[pl.BlockSpec((tm, d), lambda i, f: (i, 0)),
                  pl.BlockSpec((d, tf), lambda i, f: (0, f)),
                  pl.BlockSpec((d, tf), lambda i, f: (0, f)),
                  pl.BlockSpec((tf, d), lambda i, f: (f, 0)),
                  pl.BlockSpec((tm, d), lambda i, f: (i, 0)),
                  pl.BlockSpec((1, N_MOD, d), mod_index)],
        out_specs=pl.BlockSpec((tm, d), lambda i, f: (i, 0)),
        out_shape=jax.ShapeDtypeStruct((t, d), F32),
        scratch_shapes=[pltpu.VMEM((tm, d), F32)],
        compiler_params=_params(("arbitrary", "arbitrary")),
        name="ffn",
    )(h, wg, wu, wd, xs, mods)


def _rope(x, c, s):
    lane = lax.broadcasted_iota(jnp.int32, x.shape, 1)
    partner = jnp.where(lane % 32 < 16, pltpu.roll(x, LANES - 16, 1), pltpu.roll(x, 16, 1))
    return x * c + partner * s


def _mla_kernel(x_ref, mod_ref, g_ref, win_ref, qg_ref, wq_ref, kvg_ref, wkv_ref,
                cos_ref, sin_ref, slg_ref, slb_ref, sw_ref, sb_ref,
                q_ref, k_ref, v_ref, sg_ref, *, n_lat_tiles):
    i = pl.program_id(0)
    tm = x_ref.shape[0]
    h = _rmsnorm_mod(x_ref[...], g_ref[...], mod_ref[0, 0:1, :], mod_ref[0, 1:2, :]).astype(BF16)
    p = jnp.dot(h, win_ref[...], preferred_element_type=F32)
    c0 = Q_LORA
    c1 = c0 + KV_LORA
    d_sgu = slg_ref.shape[1]
    c2 = c1 + d_sgu
    c3 = c2 + d_sgu
    cos = cos_ref[...]
    sin = sin_ref[...]

    def rms(x, g):
        return x * lax.rsqrt(jnp.mean(x * x, axis=-1, keepdims=True) + EPS) * g

    ckv = rms(p[:, c0:c1], kvg_ref[...]).astype(BF16)
    kv = jnp.dot(ckv, wkv_ref[...], preferred_element_type=F32)
    kr = _rope(p[:, c3:c3 + LANES], cos, sin).astype(BF16)
    nk = MLA_HEADS * QK_NOPE
    for hd in range(MLA_HEADS):
        k_ref[:, hd * HEAD_PAD:hd * HEAD_PAD + QK_NOPE] = kv[:, hd * QK_NOPE:(hd + 1) * QK_NOPE].astype(BF16)
        k_ref[:, hd * HEAD_PAD + QK_NOPE:(hd + 1) * HEAD_PAD] = kr
        v_ref[:, hd * HEAD_PAD:hd * HEAD_PAD + V_HEAD] = kv[:, nk + hd * V_HEAD:nk + (hd + 1) * V_HEAD].astype(BF16)
        v_ref[:, hd * HEAD_PAD + V_HEAD:(hd + 1) * HEAD_PAD] = jnp.ones((tm, HEAD_PAD - V_HEAD), BF16)

    @pl.when(i < n_lat_tiles)
    def _():
        cq = rms(p[:, :c0], qg_ref[...]).astype(BF16)
        q = jnp.dot(cq, wq_ref[...], preferred_element_type=F32) * (SM_SCALE * LOG2E)
        for hd in range(MLA_HEADS):
            b0 = hd * HEAD_PAD
            q_ref[:, b0:b0 + QK_NOPE] = q[:, b0:b0 + QK_NOPE].astype(BF16)
            q_ref[:, b0 + QK_NOPE:b0 + HEAD_PAD] = _rope(q[:, b0 + QK_NOPE:b0 + HEAD_PAD], cos, sin).astype(BF16)

        zu = jax.nn.gelu(p[:, c1:c2])
        zv = _layernorm(jax.nn.gelu(p[:, c2:c3]), slg_ref[...], slb_ref[...]).astype(BF16)
        hdim = d_sgu // SGU_HEADS
        for ck in range(tm // CHUNK):
            rows = slice(ck * CHUNK, (ck + 1) * CHUNK)
            for hd in range(SGU_HEADS):
                cols = slice(hd * hdim, (hd + 1) * hdim)
                mixed = jnp.dot(sw_ref[hd], zv[rows, cols], preferred_element_type=F32) + sb_ref[:, cols]
                sg_ref[rows, cols] = (zu[rows, cols] * mixed).astype(BF16)


def _mla_proj(xs, mods, mod_index, g, w_in, qg, wq, kvg, wkv, cos_t, sin_t, slg, slb, sw, sb,
              n_lat_rows, seq, tm=MM_TILE):
    t, d = xs.shape
    n_lat_tiles = n_lat_rows // tm
    seq_tiles = seq // tm
    d_sgu = slg.shape[0]

    def lat(i):
        return (jnp.minimum(i, n_lat_tiles - 1), 0)

    def table(i):
        return (jnp.where(i < n_lat_tiles, i % seq_tiles, seq_tiles), 0)

    kern = functools.partial(_mla_kernel, n_lat_tiles=n_lat_tiles)
    return pl.pallas_call(
        kern,
        grid=(t // tm,),
        in_specs=[pl.BlockSpec((tm, d), lambda i: (i, 0)),
                  pl.BlockSpec((1, N_MOD, d), mod_index),
                  _resident((1, d)), _resident(w_in.shape),
                  _resident((1, Q_LORA)), _resident(wq.shape),
                  _resident((1, KV_LORA)), _resident(wkv.shape),
                  pl.BlockSpec((tm, LANES), table), pl.BlockSpec((tm, LANES), table),
                  _resident((1, d_sgu)), _resident((1, d_sgu)), _resident(sw.shape), _resident(sb.shape)],
        out_specs=[pl.BlockSpec((tm, MLA_HEADS * HEAD_PAD), lat),
                   pl.BlockSpec((tm, MLA_HEADS * HEAD_PAD), lambda i: (i, 0)),
                   pl.BlockSpec((tm, MLA_HEADS * HEAD_PAD), lambda i: (i, 0)),
                   pl.BlockSpec((tm, d_sgu), lat)],
        out_shape=[jax.ShapeDtypeStruct((n_lat_rows, MLA_HEADS * HEAD_PAD), BF16),
                   jax.ShapeDtypeStruct((t, MLA_HEADS * HEAD_PAD), BF16),
                   jax.ShapeDtypeStruct((t, MLA_HEADS * HEAD_PAD), BF16),
                   jax.ShapeDtypeStruct((n_lat_rows, d_sgu), BF16)],
        compiler_params=_params(("arbitrary",)),
        name="mla_proj",
    )(xs, mods, g.reshape(1, d), w_in, qg.reshape(1, -1), wq, kvg.reshape(1, -1), wkv,
      cos_t, sin_t, slg.reshape(1, -1), slb.reshape(1, -1), sw, sb)


def _attn_kernel(*refs, n_cast):
    q_ref, kc_ref, kl_ref, vc_ref, vl_ref = refs[:5]
    cast_in = refs[5:5 + n_cast]
    o_ref = refs[5 + n_cast]
    cast_out = refs[6 + n_cast:]
    dn = (((1,), (1,)), ((), ()))

    def scores(hd):
        qk = slice(hd * HEAD_PAD, (hd + 1) * HEAD_PAD)
        q = q_ref[:, qk]
        return (lax.dot_general(q, kc_ref[:, qk], dn, preferred_element_type=F32),
                lax.dot_general(q, kl_ref[:, qk], dn, preferred_element_type=F32))

    def finish(hd, sc, sl):
        vv = slice(hd * HEAD_PAD, (hd + 1) * HEAD_PAD)
        m = jnp.maximum(jnp.max(sc, axis=-1, keepdims=True), jnp.max(sl, axis=-1, keepdims=True))
        pc = jnp.exp2((sc - m).astype(BF16))
        pq = jnp.exp2((sl - m).astype(BF16))
        o = (jnp.dot(pc, vc_ref[:, vv], preferred_element_type=F32)
             + jnp.dot(pq, vl_ref[:, vv], preferred_element_type=F32))
        out = slice(hd * V_HEAD, (hd + 1) * V_HEAD)
        o_ref[:, out] = (o[:, :V_HEAD] / o[:, V_HEAD:V_HEAD + 1]).astype(o_ref.dtype)

    s_next = scores(0)
    for hd in range(ATTN_HEADS_PER_STEP):
        s_cur = s_next
        if hd + 1 < ATTN_HEADS_PER_STEP:
            s_next = scores(hd + 1)
        finish(hd, *s_cur)
    for w_ref, wb_ref in zip(cast_in, cast_out):
        wb_ref[...] = w_ref[...].astype(BF16)


def _attention(q, k, v, n_batch, seq, ctx_len, casts, tq=ROW_TILE):
    n_lat_rows = n_batch * seq
    qb = seq // tq
    ctx_blk0 = n_lat_rows // ctx_len
    hs = ATTN_HEADS_PER_STEP
    hp = MLA_HEADS // hs
    n_steps = n_batch * hp * qb

    cast_specs = []
    for w in casts:
        n_exp, rows, cols = w.shape
        per_expert = n_steps // n_exp
        assert n_steps % n_exp == 0 and rows % (per_expert * 16) == 0

        def slab(b, h, j, per_expert=per_expert):
            step = (b * hp + h) * qb + j
            return (step // per_expert, step % per_expert, 0)
        cast_specs.append(pl.BlockSpec((1, rows // per_expert, cols), slab))
    outs = pl.pallas_call(
        functools.partial(_attn_kernel, n_cast=len(casts)),
        grid=(n_batch, hp, qb),
        in_specs=[pl.BlockSpec((tq, hs * HEAD_PAD), lambda b, h, j: (b * qb + j, h)),
                  pl.BlockSpec((ctx_len, hs * HEAD_PAD), lambda b, h, j: (ctx_blk0 + b, h)),
                  pl.BlockSpec((seq, hs * HEAD_PAD), lambda b, h, j: (b, h)),
                  pl.BlockSpec((ctx_len, hs * HEAD_PAD), lambda b, h, j: (ctx_blk0 + b, h)),
                  pl.BlockSpec((seq, hs * HEAD_PAD), lambda b, h, j: (b, h))] + cast_specs,
        out_specs=[pl.BlockSpec((tq, hs * V_HEAD), lambda b, h, j: (b * qb + j, h))] + cast_specs,
        out_shape=[jax.ShapeDtypeStruct((n_lat_rows, MLA_HEADS * V_HEAD), BF16)]
        + [jax.ShapeDtypeStruct(w.shape, BF16) for w in casts],
        compiler_params=_params(("arbitrary", "arbitrary", "arbitrary")),
        name="attention",
    )(q, k, k, v, v, *casts)
    return outs[0], outs[1:]


SRC_BITS = 15


def _moe_kernel(te_ref, nact_ref, route_ref, h_hbm, wg_ref, wu_ref, wd_ref, y_hbm,
                xbuf, xb, acc, gsem, ssem):
    i = pl.program_id(0)
    f = pl.program_id(1)
    tm = xbuf.shape[0]
    chunk = tm // (MOE_NF + 1)
    nact = nact_ref[0]
    active = i < nact
    slot = i % 2
    dump0 = y_hbm.shape[0] - tm

    def gather_copy(tile, r):
        src = route_ref[(tile + 1) * tm + r] & ((1 << SRC_BITS) - 1)
        return pltpu.make_async_copy(h_hbm.at[pl.ds(src, 1)], xbuf.at[pl.ds(r, 1)], gsem)

    def scatter_copy(tile, r):
        dst = route_ref[(tile + 1) * tm + r] >> SRC_BITS
        return pltpu.make_async_copy(acc.at[tile % 2, pl.ds(r, 1)], y_hbm.at[pl.ds(dst, 1)], ssem)

    def wait_gather():
        pltpu.make_async_copy(h_hbm.at[pl.ds(0, tm)], xbuf, gsem).wait()

    def wait_scatter():
        pltpu.make_async_copy(acc.at[0], y_hbm.at[pl.ds(0, tm)], ssem).wait()

    def start_all(copy, tile):
        def body(r, c):
            copy(tile, r).start()
            return c
        lax.fori_loop(0, tm, body, 0)

    @pl.when((i == 0) & (f == 0))
    def _():
        acc[1] = jnp.zeros(acc.shape[1:], F32)
        fill = pltpu.make_async_copy(acc.at[1], y_hbm.at[pl.ds(dump0, tm)], ssem)
        fill.start()
        fill.wait()
        start_all(gather_copy, 0)

    @pl.when((f == 0) & (i <= nact))
    def _():
        wait_gather()

        @pl.when(i >= 1)
        def _():
            wait_scatter()

    def issue_chunk(c):
        for j in range(chunk):
            r = c * chunk + j
            gather_copy(i + 1, r).start()
            scatter_copy(i - 1, r).start()

    @pl.when((f == 0) & active)
    def _():
        xb[...] = xbuf[...].astype(BF16)
        acc[slot] = jnp.zeros(acc.shape[1:], F32)
        issue_chunk(0)

    @pl.when((f == 0) & (i == nact) & (i >= 1))
    def _():
        start_all(scatter_copy, i - 1)
        wait_scatter()

    @pl.when(active)
    def _():
        issue_chunk(f + 1)
        x = xb[...]
        a = jnp.dot(x, wg_ref[0], preferred_element_type=F32)
        b = jnp.dot(x, wu_ref[0], preferred_element_type=F32)
        acc[slot] += jnp.dot((_silu(a) * b).astype(BF16), wd_ref[0], preferred_element_type=F32)


def _moe(h, wg, wu, wd, tile_expert, n_active, route, n_out_rows, tm=MOE_TM):
    d = h.shape[1]
    ff = wg.shape[2]
    n_tiles = tile_expert.shape[0]
    nf = MOE_NF
    tf = ff // nf

    def fidx(i, f, nact):
        return jnp.where(i < nact[0], f, nf - 1)

    grid_spec = pltpu.PrefetchScalarGridSpec(
        num_scalar_prefetch=3,
        grid=(n_tiles, nf),
        in_specs=[pl.BlockSpec(memory_space=pl.ANY),
                  pl.BlockSpec((1, d, tf), lambda i, f, te, nact, route: (te[i], 0, fidx(i, f, nact))),
                  pl.BlockSpec((1, d, tf), lambda i, f, te, nact, route: (te[i], 0, fidx(i, f, nact))),
                  pl.BlockSpec((1, tf, d), lambda i, f, te, nact, route: (te[i], fidx(i, f, nact), 0))],
        out_specs=pl.BlockSpec(memory_space=pl.ANY),
        scratch_shapes=[pltpu.VMEM((tm, d), F32), pltpu.VMEM((tm, d), BF16), pltpu.VMEM((2, tm, d), F32),
                        pltpu.SemaphoreType.DMA, pltpu.SemaphoreType.DMA],
    )
    return pl.pallas_call(
        _moe_kernel,
        grid_spec=grid_spec,
        out_shape=jax.ShapeDtypeStruct((n_out_rows, d), F32),
        compiler_params=_params(("arbitrary", "arbitrary")),
        name="moe",
    )(tile_expert, n_active, route, h, wg, wu, wd)


def _route_plan(r, n_tok, tm):
    n_assign = 2 * n_tok
    n_tiles = n_assign // tm + N_EXPERTS
    e_flat = jnp.concatenate([r[:, 0], r[:, 1]]).astype(jnp.int32)
    experts = jnp.arange(N_EXPERTS, dtype=jnp.int32)
    counts = jnp.sum(e_flat[:, None] == experts[None, :], axis=0).astype(jnp.int32)
    tiles_per = (counts + tm - 1) // tm
    tend = jnp.cumsum(tiles_per)
    n_active = tend[-1]
    j = jnp.arange(n_tiles, dtype=jnp.int32)
    te = jnp.minimum(jnp.sum(j[:, None] >= tend[None, :], axis=1), N_EXPERTS - 1).astype(jnp.int32)
    last = jnp.sum(jnp.where(j == n_active - 1, te, 0))
    te = jnp.where(j < n_active, te, last)
    pad_id = jnp.arange(tm, dtype=jnp.int32)
    pad_on = pad_id[None, :] < (tiles_per * tm - counts)[:, None]
    pad_key = jnp.where(pad_on, 2 * experts[:, None] + 1, 2 * N_EXPERTS).reshape(-1)
    keys = jnp.concatenate([2 * e_flat, pad_key])
    item = jnp.argsort(keys, stable=True).astype(jnp.int32)
    row = jnp.arange(n_tiles * tm, dtype=jnp.int32)
    real = item < n_assign
    src = jnp.where(real, item % n_tok, 0)
    dst = jnp.where(real, item, n_assign + row % tm)
    body = src | (dst << SRC_BITS)
    edge = (jnp.arange(tm, dtype=jnp.int32) + n_assign) << SRC_BITS
    route = jnp.concatenate([edge, body, edge]).astype(jnp.int32)
    return te, n_active.reshape(1).astype(jnp.int32), route


def _combine_kernel(x_ref, y0_ref, y1_ref, r_ref, mod_ref, g_ref, o_ref):
    r = r_ref[...]
    y = r[:, 2:3] * y0_ref[...] + r[:, 3:4] * y1_ref[...]
    x = x_ref[...] + mod_ref[0, 5:6, :] * y
    ms = jnp.mean(x * x, axis=-1, keepdims=True)
    o_ref[...] = x * lax.rsqrt(ms + EPS) * g_ref[...]


def _combine(xs, y, r, mods, mod_index, g, tm=MM_TILE):
    t, d = xs.shape
    nb = t // tm
    return pl.pallas_call(
        _combine_kernel,
        grid=(nb,),
        in_specs=[pl.BlockSpec((tm, d), lambda i: (i, 0)),
                  pl.BlockSpec((tm, d), lambda i: (i, 0)),
                  pl.BlockSpec((tm, d), lambda i: (nb + i, 0)),
                  pl.BlockSpec((tm, LANES), lambda i: (i, 0)),
                  pl.BlockSpec((1, N_MOD, d), mod_index),
                  _resident((1, d))],
        out_specs=pl.BlockSpec((tm, d), lambda i: (i, 0)),
        out_shape=jax.ShapeDtypeStruct((t, d), F32),
        compiler_params=_params(("arbitrary",)),
        name="combine",
    )(xs, y, y, r, mods, g.reshape(1, d))


def _rope_tables(seq, extra_rows):
    rows = seq // GRID_W
    row = jnp.repeat(jnp.arange(rows), GRID_W).astype(F32)
    col = jnp.tile(jnp.arange(GRID_W), rows).astype(F32)
    inv = ROPE_THETA ** (-jnp.arange(ROPE_FREQS, dtype=F32) / ROPE_FREQS)
    ar = row[:, None] * inv
    ac = col[:, None] * inv
    cos = jnp.concatenate([jnp.cos(ar), jnp.cos(ar), jnp.cos(ac), jnp.cos(ac)], axis=1)
    sin = jnp.concatenate([-jnp.sin(ar), jnp.sin(ar), -jnp.sin(ac), jnp.sin(ac)], axis=1)
    pad = LANES - QK_ROPE
    cos = jnp.concatenate([cos, jnp.ones((seq, pad), F32)], axis=1)
    sin = jnp.concatenate([sin, jnp.zeros((seq, pad), F32)], axis=1)
    cos = jnp.concatenate([cos, jnp.ones((extra_rows, LANES), F32)], axis=0)
    sin = jnp.concatenate([sin, jnp.zeros((extra_rows, LANES), F32)], axis=0)
    return cos, sin


def kernel(x, c, ctx, c_ctx, ada_w, ada_b, norm1_g, norm2_g, e_w_in, e_pool_w, e_pool_scale, e_conv_w, e_conv_b, e_conv_ln_g, e_conv_ln_b, e_w_out, e_ffn_w_gate, e_ffn_w_up, e_ffn_w_down, o_w_in, o_q_norm_g, o_w_qb, o_kv_norm_g, o_w_kvb, o_sgu_ln_g, o_sgu_ln_b, o_sgu_w, o_sgu_b, o_w_out, o_router_w, o_router_b, o_exp_w_gate, o_exp_w_up, o_exp_w_down, final_norm_g):
    n_batch, seq, d = x.shape
    ctx_len = ctx.shape[1]
    n_lat = n_batch * seq
    n_ctx = n_batch * ctx_len
    assert ctx_len % ROW_TILE == 0 and seq % MM_TILE == 0 and n_ctx % MM_TILE == 0
    assert ada_w.shape[0] == 2 and e_w_in.shape[0] == 1 and o_w_in.shape[0] == 1

    rows_per_layer = 8 * ((n_batch + 1 + 7) // 8)
    cond = jnp.zeros((rows_per_layer, d), F32).at[:n_batch].set(c).at[n_batch].set(c_ctx)
    mods = _ada(cond, ada_w, ada_b).reshape(2 * rows_per_layer, N_MOD, d)

    def mod_index(layer, tm):
        return _mod_spec(layer, tm, n_lat, seq, n_batch, rows_per_layer)

    x_lat = x.reshape(n_lat, d)
    x_ctx = ctx.reshape(n_ctx, d)

    p = _inproj(x_lat, x_ctx, mods, mod_index(0, MM_TILE), norm1_g[0], e_w_in[0].astype(BF16))
    y = _poolconv(p, e_pool_w[0].astype(BF16), e_pool_scale[0], e_conv_w[0], e_conv_b[0],
                  e_conv_ln_g[0], e_conv_ln_b[0], n_lat, seq, ctx_len)
    x1, h2 = _outproj([y], e_w_out[0].astype(BF16), (x_lat, x_ctx), mods, mod_index(0, MM_TILE), norm2_g[0],
                      n_lat + n_ctx, BF16)
    x2 = _ffn(h2, e_ffn_w_gate[0].astype(BF16), e_ffn_w_up[0].astype(BF16),
              e_ffn_w_down[0].astype(BF16), x1, mods, mod_index(0, MM_TILE))

    w_in = o_w_in[0]
    c_kr = Q_LORA + KV_LORA
    c_u = c_kr + QK_ROPE
    w_in_r = jnp.concatenate(
        [w_in[:, :c_kr], w_in[:, c_u:], w_in[:, c_kr:c_u], jnp.zeros((d, LANES - QK_ROPE), F32)],
        axis=1).astype(BF16)
    wq = o_w_qb[0].reshape(Q_LORA, MLA_HEADS, QK_NOPE + QK_ROPE)
    wq = jnp.pad(wq, ((0, 0), (0, 0), (0, HEAD_PAD - QK_NOPE - QK_ROPE)))
    wq = wq.reshape(Q_LORA, MLA_HEADS * HEAD_PAD).astype(BF16)
    wkv = o_w_kvb[0].reshape(KV_LORA, MLA_HEADS, QK_NOPE + V_HEAD)
    wkv = jnp.concatenate([wkv[:, :, :QK_NOPE].reshape(KV_LORA, -1),
                           wkv[:, :, QK_NOPE:].reshape(KV_LORA, -1)], axis=1).astype(BF16)
    cos_t, sin_t = _rope_tables(seq, MM_TILE)
    d_sgu = o_sgu_ln_g.shape[1]
    sgu_bias = jnp.repeat(o_sgu_b[0].T, d_sgu // SGU_HEADS, axis=1)
    q, k, v, sg = _mla_proj(x2, mods, mod_index(1, MM_TILE), norm1_g[1], w_in_r,
                            o_q_norm_g[0], wq, o_kv_norm_g[0], wkv, cos_t, sin_t,
                            o_sgu_ln_g[0], o_sgu_ln_b[0], o_sgu_w[0].astype(BF16), sgu_bias,
                            n_lat, seq)
    experts = (o_exp_w_gate[0], o_exp_w_up[0], o_exp_w_down[0])
    attn, (wg, wu, wd) = _attention(q, k, v, n_batch, seq, ctx_len, experts)

    rw = jnp.pad(o_router_w[0], ((0, 0), (0, LANES - N_EXPERTS))).astype(BF16)
    rb = jnp.concatenate([o_router_b[0], jnp.full((LANES - N_EXPERTS,), NEG, F32)]).reshape(1, LANES)
    x3, h3, r = _outproj([attn, sg], o_w_out[0].astype(BF16), (x2,), mods, mod_index(1, MM_TILE),
                         norm2_g[1], n_lat, F32, router=(rw, rb))

    te, n_active, route = _route_plan(r, n_lat, MOE_TM)
    ys = _moe(h3, wg, wu, wd, te, n_active, route, 2 * n_lat + MOE_TM)
    out = _combine(x3, ys, r, mods, mod_index(1, MM_TILE), final_norm_g)
    return out.reshape(n_batch, seq, d)
```

```python
import functools

import jax
import jax.numpy as jnp
import numpy as np
from jax import lax
from jax.experimental import pallas as pl
from jax.experimental.pallas import tpu as pltpu

F32 = jnp.float32
BF16 = jnp.bfloat16

EPS = 1e-6
N_MOD = 6
GRID_W = 64
POOL_WINDOWS = (2, 4, 8, 16)
CONV_WIDTH = 31
MLA_HEADS = 8
Q_LORA = 512
KV_LORA = 512
QK_NOPE = 128
QK_ROPE = 64
V_HEAD = 128
ROPE_FREQS = QK_ROPE // 4
ROPE_THETA = 10000.0
SM_SCALE = (QK_NOPE + QK_ROPE) ** -0.5
LOG2E = 1.4426950408889634
SGU_HEADS = 8
CHUNK = 128
N_EXPERTS = 8

LANES = 128
SUBLANES = 8
BF16_ROWS = 16
HALO = 16
VMEM_LIMIT = 56 * 1024 * 1024
HEAD_PAD = 256
ROW_TILE = 256
MM_TILE = 512
FFN_TF = 512
MOE_TM = 512
MOE_NF = 7
ATTN_HEADS_PER_STEP = 2
NEG = -1e30


def _params(sem):
    return pltpu.CompilerParams(dimension_semantics=sem, vmem_limit_bytes=VMEM_LIMIT)


def _resident(shape):
    nd = len(shape)
    return pl.BlockSpec(shape, lambda *_: (0,) * nd, pipeline_mode=pl.Buffered(1))


def _slab_specs(arrays, n_steps, step_of):
    specs = []
    for w in arrays:
        rows, cols = w.shape
        n = max(k for k in range(1, n_steps + 1) if rows % (BF16_ROWS * k) == 0)
        specs.append(pl.BlockSpec((rows // n, cols),
                                  lambda *idx, n=n: (jnp.minimum(step_of(*idx), n - 1), 0)))
    return specs


def _cast_slabs(in_refs, out_refs):
    for w_ref, wb_ref in zip(in_refs, out_refs):
        wb_ref[...] = w_ref[...].astype(BF16)


def _rmsnorm_mod(x, g, shift, scale):
    ms = jnp.mean(x * x, axis=-1, keepdims=True)
    return (x * lax.rsqrt(ms + EPS) * g) * (1.0 + scale) + shift


def _layernorm(x, g, b):
    mu = jnp.mean(x, axis=-1, keepdims=True)
    xc = x - mu
    var = jnp.mean(xc * xc, axis=-1, keepdims=True)
    return xc * lax.rsqrt(var + EPS) * g + b


def _silu(x):
    return x * jax.nn.sigmoid(x)


def _ada_kernel(c_ref, w_ref, b_ref, o_ref):
    s = _silu(c_ref[...]).astype(BF16)
    o_ref[0] = jnp.dot(s, w_ref[0].astype(BF16), preferred_element_type=F32) + b_ref[0]


def _ada(cond, ada_w, ada_b, tn=1024):
    depth, d, n = ada_w.shape
    rows = cond.shape[0]
    return pl.pallas_call(
        _ada_kernel,
        grid=(depth, n // tn),
        in_specs=[pl.BlockSpec((rows, d), lambda l, j: (0, 0)),
                  pl.BlockSpec((1, d, tn), lambda l, j: (l, 0, j)),
                  pl.BlockSpec((1, 1, tn), lambda l, j: (l, 0, j))],
        out_specs=pl.BlockSpec((1, rows, tn), lambda l, j: (l, 0, j)),
        out_shape=jax.ShapeDtypeStruct((depth, rows, n), F32),
        compiler_params=_params(("arbitrary", "arbitrary")),
        name="ada",
    )(cond, ada_w, ada_b.reshape(depth, 1, n))


def _mod_spec(layer, tm, n_lat_rows, seq, n_batch, rows_per_layer):
    def index(i, *_):
        r = jnp.where(i * tm < n_lat_rows, (i * tm) // seq, n_batch)
        return (layer * rows_per_layer + r, 0, 0)
    return index


def _pick_rows(i, n_lat_tiles, lat_ref, ctx_ref):
    return jnp.where(i < n_lat_tiles, lat_ref[...], ctx_ref[...])


def _two_source_specs(tm, d, n_lat_tiles):
    return [pl.BlockSpec((tm, d), lambda i: (jnp.minimum(i, n_lat_tiles - 1), 0)),
            pl.BlockSpec((tm, d), lambda i: (jnp.maximum(i - n_lat_tiles, 0), 0))]


def _inproj_kernel(*refs, n_lat_tiles, n_cast):
    xl_ref, xc_ref, mod_ref, g_ref, w_ref = refs[:5]
    o_ref = refs[5 + n_cast]
    x = _pick_rows(pl.program_id(0), n_lat_tiles, xl_ref, xc_ref)
    h = _rmsnorm_mod(x, g_ref[...], mod_ref[0, 0:1, :], mod_ref[0, 1:2, :])
    o_ref[...] = jnp.dot(h.astype(BF16), w_ref[...], preferred_element_type=F32).astype(o_ref.dtype)
    _cast_slabs(refs[5:5 + n_cast], refs[6 + n_cast:])


def _inproj(x_lat, x_ctx, mods, mod_index, g, w, casts, tm=MM_TILE):
    d = x_lat.shape[1]
    t = x_lat.shape[0] + x_ctx.shape[0]
    n = w.shape[1]
    n_lat_tiles = x_lat.shape[0] // tm
    cast_specs = _slab_specs(casts, t // tm, lambda i: i)
    outs = pl.pallas_call(
        functools.partial(_inproj_kernel, n_lat_tiles=n_lat_tiles, n_cast=len(casts)),
        grid=(t // tm,),
        in_specs=_two_source_specs(tm, d, n_lat_tiles) + [
            pl.BlockSpec((1, N_MOD, d), mod_index), _resident((1, d)), _resident((d, n))] + cast_specs,
        out_specs=[pl.BlockSpec((tm, n), lambda i: (i, 0))] + cast_specs,
        out_shape=[jax.ShapeDtypeStruct((t, n), BF16)] + [jax.ShapeDtypeStruct(c.shape, BF16) for c in casts],
        compiler_params=_params(("arbitrary",)),
        name="inproj0",
    )(x_lat, x_ctx, mods, g.reshape(1, d), w, *casts)
    return outs[0], outs[1:]


def _poolconv_kernel(up_ref, um_ref, un_ref, ap_ref, am_ref, an_ref, gp_ref, gm_ref, gn_ref,
                     pw_ref, ps_ref, cw_ref, cb_ref, lg_ref, lb_ref, y_ref,
                     ubuf, zbuf, zsh, cbuf, *, n_lat_tiles, seq, ctx_len):
    i = pl.program_id(0)
    tm = um_ref.shape[0]
    dp = um_ref.shape[1]
    is_lat = i < n_lat_tiles
    pos0 = jnp.where(is_lat, (i * tm) % seq, ((i - n_lat_tiles) * tm) % ctx_len)
    length = jnp.where(is_lat, seq, ctx_len)
    keep_p = jnp.where(pos0 == 0, 0.0, 1.0).astype(F32)
    keep_n = jnp.where(pos0 + tm == length, 0.0, 1.0).astype(F32)

    def glu(a_ref, g_ref):
        return a_ref[...].astype(F32) * jax.nn.sigmoid(g_ref[...].astype(F32))

    ubuf[0:HALO, :] = up_ref[...].astype(F32) * keep_p
    ubuf[HALO:HALO + tm, :] = um_ref[...].astype(F32)
    ubuf[HALO + tm:, :] = un_ref[...].astype(F32) * keep_n
    zbuf[0:HALO, :] = glu(ap_ref, gp_ref) * keep_p
    zbuf[HALO:HALO + tm, :] = glu(am_ref, gm_ref)
    zbuf[HALO + tm:, :] = glu(an_ref, gn_ref) * keep_n

    pos = pos0 + lax.broadcasted_iota(jnp.int32, (tm, 1), 0)
    gdim = dp // len(POOL_WINDOWS)
    for g, w in enumerate(POOL_WINDOWS):
        cols = slice(g * gdim, (g + 1) * gdim)
        s = ubuf[HALO - w // 2:HALO - w // 2 + tm, cols]
        for o in range(-w // 2 + 1, w // 2):
            s = s + ubuf[HALO + o:HALO + o + tm, cols]
        cnt = jnp.minimum(pos - w // 2 + w, length) - jnp.maximum(pos - w // 2, 0)
        pooled = s / cnt.astype(F32) - ubuf[HALO:HALO + tm, cols]
        mixed = jnp.dot(pooled.astype(BF16), pw_ref[g], preferred_element_type=F32)
        y_ref[:, cols] = (mixed * ps_ref[:, cols]).astype(y_ref.dtype)

    n_sh = zsh.shape[1]
    for j in range(1, SUBLANES):
        zsh[j - 1] = zbuf[j:j + n_sh, :]
    rb = 64
    first = HALO - CONV_WIDTH // 2
    for c in range(0, zbuf.shape[1], LANES):
        cols = slice(c, c + LANES)
        taps = [cw_ref[k:k + 1, cols] for k in range(CONV_WIDTH)]
        for r in range(0, tm, rb):
            acc = None
            for k in range(CONV_WIDTH):
                q, j = divmod(first + k, SUBLANES)
                rows = slice(q * SUBLANES + r, q * SUBLANES + r + rb)
                z = zbuf[rows, cols] if j == 0 else zsh[j - 1, rows, cols]
                acc = taps[k] * z if acc is None else acc + taps[k] * z
            cbuf[r:r + rb, cols] = acc
    conv = _layernorm(cbuf[...] + cb_ref[...], lg_ref[...], lb_ref[...])
    y_ref[:, dp:] = _silu(conv).astype(y_ref.dtype)


def _poolconv(p, pool_w, pool_scale, conv_w, conv_b, ln_g, ln_b, n_lat_rows, seq, ctx_len,
              tm=ROW_TILE):
    t = p.shape[0]
    dp = pool_scale.shape[0]
    dc = conv_b.shape[0]
    hb = tm // HALO
    last_hb = t // HALO - 1

    def main(c):
        return pl.BlockSpec((tm, dp), lambda i: (i, c))

    def prev(c):
        return pl.BlockSpec((HALO, dp), lambda i: (jnp.maximum(i * hb - 1, 0), c))

    def nxt(c):
        return pl.BlockSpec((HALO, dp), lambda i: (jnp.minimum((i + 1) * hb, last_hb), c))

    kern = functools.partial(_poolconv_kernel, n_lat_tiles=n_lat_rows // tm, seq=seq, ctx_len=ctx_len)
    return pl.pallas_call(
        kern,
        grid=(t // tm,),
        in_specs=[prev(0), main(0), nxt(0), prev(1), main(1), nxt(1), prev(2), main(2), nxt(2),
                  _resident(pool_w.shape), _resident((1, dp)), _resident(conv_w.shape),
                  _resident((1, dc)), _resident((1, dc)), _resident((1, dc))],
        out_specs=pl.BlockSpec((tm, dp + dc), lambda i: (i, 0)),
        out_shape=jax.ShapeDtypeStruct((t, dp + dc), BF16),
        scratch_shapes=[pltpu.VMEM((tm + 2 * HALO, dp), F32),
                        pltpu.VMEM((tm + 2 * HALO, dc), F32),
                        pltpu.VMEM((SUBLANES - 1, tm + 2 * HALO - SUBLANES, dc), F32),
                        pltpu.VMEM((tm, dc), F32)],
        compiler_params=_params(("arbitrary",)),
        name="poolconv",
    )(p, p, p, p, p, p, p, p, p, pool_w, pool_scale.reshape(1, dp), conv_w,
      conv_b.reshape(1, dc), ln_g.reshape(1, dc), ln_b.reshape(1, dc))


def _outproj_kernel(*refs, n_in, n_res, n_lat_tiles, router):
    y_refs = refs[:n_in]
    w_ref = refs[n_in]
    x_refs = refs[n_in + 1:n_in + 1 + n_res]
    mod_ref, g_ref = refs[n_in + 1 + n_res:n_in + 3 + n_res]
    rest = refs[n_in + 3 + n_res:]
    if router:
        rw_ref, rb_ref, xo_ref, ho_ref, r_ref = rest
    else:
        xo_ref, ho_ref = rest
    o = None
    k0 = 0
    for y_ref in y_refs:
        kk = y_ref.shape[1]
        part = jnp.dot(y_ref[...], w_ref[k0:k0 + kk, :], preferred_element_type=F32)
        o = part if o is None else o + part
        k0 += kk
    if n_res == 2:
        x = _pick_rows(pl.program_id(0), n_lat_tiles, x_refs[0], x_refs[1])
    else:
        x = x_refs[0][...]
    x1 = x + mod_ref[0, 2:3, :] * o
    xo_ref[...] = x1
    h = _rmsnorm_mod(x1, g_ref[...], mod_ref[0, 3:4, :], mod_ref[0, 4:5, :])
    ho_ref[...] = h.astype(ho_ref.dtype)
    if router:
        logits = jnp.dot(h.astype(BF16), rw_ref[...], preferred_element_type=F32) + rb_ref[...]
        lane = lax.broadcasted_iota(jnp.int32, logits.shape, 1)
        m1 = jnp.max(logits, axis=-1, keepdims=True)
        i1 = jnp.min(jnp.where(logits == m1, lane, LANES), axis=-1, keepdims=True)
        rest_l = jnp.where(lane == i1, NEG * 2, logits)
        m2 = jnp.max(rest_l, axis=-1, keepdims=True)
        i2 = jnp.min(jnp.where(rest_l == m2, lane, LANES), axis=-1, keepdims=True)
        e2 = jnp.exp(m2 - m1)
        p1 = 1.0 / (1.0 + e2)
        p2 = e2 / (1.0 + e2)
        r = jnp.where(lane == 0, i1.astype(F32), 0.0)
        r = jnp.where(lane == 1, i2.astype(F32), r)
        r = jnp.where(lane == 2, p1, r)
        r = jnp.where(lane == 3, p2, r)
        r_ref[...] = r


def _outproj(ys, w, res, mods, mod_index, g, n_rows, h_dtype, router=None, tm=MM_TILE):
    d = w.shape[1]
    in_specs = [pl.BlockSpec((tm, y.shape[1]), lambda i: (i, 0)) for y in ys]
    in_specs.append(_resident(w.shape))
    n_lat_tiles = res[0].shape[0] // tm
    if len(res) == 2:
        in_specs += _two_source_specs(tm, d, n_lat_tiles)
    else:
        in_specs.append(pl.BlockSpec((tm, d), lambda i: (i, 0)))
    in_specs += [pl.BlockSpec((1, N_MOD, d), mod_index), _resident((1, d))]
    args = list(ys) + [w] + list(res) + [mods, g.reshape(1, d)]
    out_specs = [pl.BlockSpec((tm, d), lambda i: (i, 0)), pl.BlockSpec((tm, d), lambda i: (i, 0))]
    out_shape = [jax.ShapeDtypeStruct((n_rows, d), F32), jax.ShapeDtypeStruct((n_rows, d), h_dtype)]
    if router is not None:
        rw, rb = router
        in_specs += [_resident(rw.shape), _resident(rb.shape)]
        args += [rw, rb]
        out_specs.append(pl.BlockSpec((tm, LANES), lambda i: (i, 0)))
        out_shape.append(jax.ShapeDtypeStruct((n_rows, LANES), F32))
    kern = functools.partial(_outproj_kernel, n_in=len(ys), n_res=len(res), n_lat_tiles=n_lat_tiles,
                             router=router is not None)
    return pl.pallas_call(
        kern,
        grid=(n_rows // tm,),
        in_specs=in_specs,
        out_specs=out_specs,
        out_shape=out_shape,
        compiler_params=_params(("arbitrary",)),
        name="outproj_router" if router is not None else "outproj",
    )(*args)


def _ffn_kernel(h_ref, wg_ref, wu_ref, wd_ref, x_ref, mod_ref, o_ref, acc_ref):
    f = pl.program_id(1)

    @pl.when(f == 0)
    def _():
        acc_ref[...] = jnp.zeros(acc_ref.shape, F32)

    h = h_ref[...]
    a = jnp.dot(h, wg_ref[...], preferred_element_type=F32)
    b = jnp.dot(h, wu_ref[...], preferred_element_type=F32)
    acc_ref[...] += jnp.dot((_silu(a) * b).astype(BF16), wd_ref[...], preferred_element_type=F32)

    @pl.when(f == pl.num_programs(1) - 1)
    def _():
        o_ref[...] = x_ref[...] + mod_ref[0, 5:6, :] * acc_ref[...]


def _ffn(h, wg, wu, wd, xs, mods, mod_index, tm=MM_TILE, tf=FFN_TF):
    t, d = xs.shape
    ff = wg.shape[1]
    return pl.pallas_call(
        _ffn_kernel,
        grid=(t // tm, ff // tf),
        in_specs=[pl.BlockSpec((tm, d), lambda i, f: (i, 0)),
                  pl.BlockSpec((d, tf), lambda i, f: (0, f)),
                  pl.BlockSpec((d, tf), lambda i, f: (0, f)),
                  pl.BlockSpec((tf, d), lambda i, f: (f, 0)),
                  pl.BlockSpec((tm, d), lambda i, f: (i, 0)),
                  pl.BlockSpec((1, N_MOD, d), mod_index)],
        out_specs=pl.BlockSpec((tm, d), lambda i, f: (i, 0)),
        out_shape=jax.ShapeDtypeStruct((t, d), F32),
        scratch_shapes=[pltpu.VMEM((tm, d), F32)],
        compiler_params=_params(("arbitrary", "arbitrary")),
        name="ffn",
    )(h, wg, wu, wd, xs, mods)


def _rope(x, c, s):
    lane = lax.broadcasted_iota(jnp.int32, x.shape, 1)
    partner = jnp.where(lane % 32 < 16, pltpu.roll(x, LANES - 16, 1), pltpu.roll(x, 16, 1))
    return x * c + partner * s


def _mla_kernel(x_ref, mod_ref, g_ref, wlora_ref, wkr_ref, wsgu_ref, qg_ref, wq_ref, kvg_ref, wkv_ref,
                cos_ref, sin_ref, slg_ref, slb_ref, sw_ref, sb_ref,
                q_ref, k_ref, v_ref, sg_ref, *, n_lat_tiles):
    i = pl.program_id(0)
    tm = x_ref.shape[0]
    h = _rmsnorm_mod(x_ref[...], g_ref[...], mod_ref[0, 0:1, :], mod_ref[0, 1:2, :]).astype(BF16)
    p_lora = jnp.dot(h, wlora_ref[...], preferred_element_type=F32)
    p_sgu = jnp.dot(h, wsgu_ref[...], preferred_element_type=F32)
    d_sgu = slg_ref.shape[1]
    cos = cos_ref[...]
    sin = sin_ref[...]

    def rms(x, g):
        return x * lax.rsqrt(jnp.mean(x * x, axis=-1, keepdims=True) + EPS) * g

    ckv = rms(p_lora[:, Q_LORA:], kvg_ref[...]).astype(BF16)
    kv = jnp.dot(ckv, wkv_ref[...], preferred_element_type=F32)
    kr = _rope(jnp.dot(h, wkr_ref[...], preferred_element_type=F32), cos, sin).astype(BF16)
    nk = MLA_HEADS * QK_NOPE
    for hd in range(MLA_HEADS):
        k_ref[:, hd * HEAD_PAD:hd * HEAD_PAD + QK_NOPE] = kv[:, hd * QK_NOPE:(hd + 1) * QK_NOPE].astype(BF16)
        k_ref[:, hd * HEAD_PAD + QK_NOPE:(hd + 1) * HEAD_PAD] = kr
        v_ref[:, hd * HEAD_PAD:hd * HEAD_PAD + V_HEAD] = kv[:, nk + hd * V_HEAD:nk + (hd + 1) * V_HEAD].astype(BF16)
        v_ref[:, hd * HEAD_PAD + V_HEAD:(hd + 1) * HEAD_PAD] = jnp.ones((tm, HEAD_PAD - V_HEAD), BF16)

    @pl.when(i < n_lat_tiles)
    def _():
        cq = rms(p_lora[:, :Q_LORA], qg_ref[...]).astype(BF16)
        q = jnp.dot(cq, wq_ref[...], preferred_element_type=F32) * (SM_SCALE * LOG2E)
        for hd in range(MLA_HEADS):
            b0 = hd * HEAD_PAD
            q_ref[:, b0:b0 + QK_NOPE] = q[:, b0:b0 + QK_NOPE].astype(BF16)
            q_ref[:, b0 + QK_NOPE:b0 + HEAD_PAD] = _rope(q[:, b0 + QK_NOPE:b0 + HEAD_PAD], cos, sin).astype(BF16)

        zu = jax.nn.gelu(p_sgu[:, :d_sgu])
        zv = _layernorm(jax.nn.gelu(p_sgu[:, d_sgu:]), slg_ref[...], slb_ref[...]).astype(BF16)
        hdim = d_sgu // SGU_HEADS
        for ck in range(tm // CHUNK):
            rows = slice(ck * CHUNK, (ck + 1) * CHUNK)
            for hd in range(SGU_HEADS):
                cols = slice(hd * hdim, (hd + 1) * hdim)
                mixed = jnp.dot(sw_ref[hd], zv[rows, cols], preferred_element_type=F32) + sb_ref[:, cols]
                sg_ref[rows, cols] = (zu[rows, cols] * mixed).astype(BF16)


def _mla_proj(xs, mods, mod_index, g, w_lora, w_kr, w_sgu, qg, wq, kvg, wkv, cos_t, sin_t, slg, slb, sw, sb,
              n_lat_rows, seq, tm=MM_TILE):
    t, d = xs.shape
    n_lat_tiles = n_lat_rows // tm
    seq_tiles = seq // tm
    d_sgu = slg.shape[0]

    def lat(i):
        return (jnp.minimum(i, n_lat_tiles - 1), 0)

    def table(i):
        return (jnp.where(i < n_lat_tiles, i % seq_tiles, seq_tiles), 0)

    kern = functools.partial(_mla_kernel, n_lat_tiles=n_lat_tiles)
    return pl.pallas_call(
        kern,
        grid=(t // tm,),
        in_specs=[pl.BlockSpec((tm, d), lambda i: (i, 0)),
                  pl.BlockSpec((1, N_MOD, d), mod_index),
                  _resident((1, d)), _resident(w_lora.shape), _resident(w_kr.shape), _resident(w_sgu.shape),
                  _resident((1, Q_LORA)), _resident(wq.shape),
                  _resident((1, KV_LORA)), _resident(wkv.shape),
                  pl.BlockSpec((tm, LANES), table), pl.BlockSpec((tm, LANES), table),
                  _resident((1, d_sgu)), _resident((1, d_sgu)), _resident(sw.shape), _resident(sb.shape)],
        out_specs=[pl.BlockSpec((tm, MLA_HEADS * HEAD_PAD), lat),
                   pl.BlockSpec((tm, MLA_HEADS * HEAD_PAD), lambda i: (i, 0)),
                   pl.BlockSpec((tm, MLA_HEADS * HEAD_PAD), lambda i: (i, 0)),
                   pl.BlockSpec((tm, d_sgu), lat)],
        out_shape=[jax.ShapeDtypeStruct((n_lat_rows, MLA_HEADS * HEAD_PAD), BF16),
                   jax.ShapeDtypeStruct((t, MLA_HEADS * HEAD_PAD), BF16),
                   jax.ShapeDtypeStruct((t, MLA_HEADS * HEAD_PAD), BF16),
                   jax.ShapeDtypeStruct((n_lat_rows, d_sgu), BF16)],
        compiler_params=_params(("arbitrary",)),
        name="mla_proj",
    )(xs, mods, g.reshape(1, d), w_lora, w_kr, w_sgu, qg.reshape(1, -1), wq, kvg.reshape(1, -1), wkv,
      cos_t, sin_t, slg.reshape(1, -1), slb.reshape(1, -1), sw, sb)


def _attn_kernel(*refs, n_cast):
    q_ref, kc_ref, kl_ref, vc_ref, vl_ref = refs[:5]
    cast_in = refs[5:5 + n_cast]
    o_ref = refs[5 + n_cast]
    cast_out = refs[6 + n_cast:]
    dn = (((1,), (1,)), ((), ()))

    def scores(hd):
        qk = slice(hd * HEAD_PAD, (hd + 1) * HEAD_PAD)
        q = q_ref[:, qk]
        return (lax.dot_general(q, kc_ref[:, qk], dn, preferred_element_type=F32),
                lax.dot_general(q, kl_ref[:, qk], dn, preferred_element_type=F32))

    def finish(hd, sc, sl):
        vv = slice(hd * HEAD_PAD, (hd + 1) * HEAD_PAD)
        m = jnp.maximum(jnp.max(sc, axis=-1, keepdims=True), jnp.max(sl, axis=-1, keepdims=True))
        pc = jnp.exp2((sc - m).astype(BF16))
        pq = jnp.exp2((sl - m).astype(BF16))
        o = (jnp.dot(pc, vc_ref[:, vv], preferred_element_type=F32)
             + jnp.dot(pq, vl_ref[:, vv], preferred_element_type=F32))
        out = slice(hd * V_HEAD, (hd + 1) * V_HEAD)
        o_ref[:, out] = (o[:, :V_HEAD] / o[:, V_HEAD:V_HEAD + 1]).astype(o_ref.dtype)

    s_next = scores(0)
    for hd in range(ATTN_HEADS_PER_STEP):
        s_cur = s_next
        if hd + 1 < ATTN_HEADS_PER_STEP:
            s_next = scores(hd + 1)
        finish(hd, *s_cur)
    _cast_slabs(cast_in, cast_out)


def _attention(q, k, v, n_batch, seq, ctx_len, casts, tq=ROW_TILE):
    n_lat_rows = n_batch * seq
    qb = seq // tq
    ctx_blk0 = n_lat_rows // ctx_len
    hs = ATTN_HEADS_PER_STEP
    hp = MLA_HEADS // hs
    n_steps = n_batch * hp * qb

    cast_specs = _slab_specs(casts, n_steps, lambda b, h, j: (b * hp + h) * qb + j)
    outs = pl.pallas_call(
        functools.partial(_attn_kernel, n_cast=len(casts)),
        grid=(n_batch, hp, qb),
        in_specs=[pl.BlockSpec((tq, hs * HEAD_PAD), lambda b, h, j: (b * qb + j, h)),
                  pl.BlockSpec((ctx_len, hs * HEAD_PAD), lambda b, h, j: (ctx_blk0 + b, h)),
                  pl.BlockSpec((seq, hs * HEAD_PAD), lambda b, h, j: (b, h)),
                  pl.BlockSpec((ctx_len, hs * HEAD_PAD), lambda b, h, j: (ctx_blk0 + b, h)),
                  pl.BlockSpec((seq, hs * HEAD_PAD), lambda b, h, j: (b, h))] + cast_specs,
        out_specs=[pl.BlockSpec((tq, hs * V_HEAD), lambda b, h, j: (b * qb + j, h))] + cast_specs,
        out_shape=[jax.ShapeDtypeStruct((n_lat_rows, MLA_HEADS * V_HEAD), BF16)]
        + [jax.ShapeDtypeStruct(w.shape, BF16) for w in casts],
        compiler_params=_params(("arbitrary", "arbitrary", "arbitrary")),
        name="attention",
    )(q, k, k, v, v, *casts)
    return outs[0], outs[1:]


SRC_BITS = 15


def _moe_kernel(te_ref, nact_ref, route_ref, h_hbm, wg_ref, wu_ref, wd_ref, y_hbm,
                xbuf, xb, acc, gsem, ssem):
    i = pl.program_id(0)
    f = pl.program_id(1)
    tm = xbuf.shape[0]
    chunk = tm // (MOE_NF + 1)
    nact = nact_ref[0]
    active = i < nact
    slot = i % 2
    dump0 = y_hbm.shape[0] - tm

    def gather_copy(tile, r):
        src = route_ref[(tile + 1) * tm + r] & ((1 << SRC_BITS) - 1)
        return pltpu.make_async_copy(h_hbm.at[pl.ds(src, 1)], xbuf.at[pl.ds(r, 1)], gsem)

    def scatter_copy(tile, r):
        dst = route_ref[(tile + 1) * tm + r] >> SRC_BITS
        return pltpu.make_async_copy(acc.at[tile % 2, pl.ds(r, 1)], y_hbm.at[pl.ds(dst, 1)], ssem)

    def wait_gather():
        pltpu.make_async_copy(h_hbm.at[pl.ds(0, tm)], xbuf, gsem).wait()

    def wait_scatter():
        pltpu.make_async_copy(acc.at[0], y_hbm.at[pl.ds(0, tm)], ssem).wait()

    def start_all(copy, tile):
        def body(r, c):
            copy(tile, r).start()
            return c
        lax.fori_loop(0, tm, body, 0)

    @pl.when((i == 0) & (f == 0))
    def _():
        acc[1] = jnp.zeros(acc.shape[1:], F32)
        fill = pltpu.make_async_copy(acc.at[1], y_hbm.at[pl.ds(dump0, tm)], ssem)
        fill.start()
        fill.wait()
        start_all(gather_copy, 0)

    @pl.when((f == 0) & (i <= nact))
    def _():
        wait_gather()

        @pl.when(i >= 1)
        def _():
            wait_scatter()

    def issue_chunk(c):
        for j in range(chunk):
            r = c * chunk + j
            gather_copy(i + 1, r).start()
            scatter_copy(i - 1, r).start()

    @pl.when((f == 0) & active)
    def _():
        xb[...] = xbuf[...].astype(BF16)
        acc[slot] = jnp.zeros(acc.shape[1:], F32)
        issue_chunk(0)

    @pl.when((f == 0) & (i == nact) & (i >= 1))
    def _():
        start_all(scatter_copy, i - 1)
        wait_scatter()

    @pl.when(active)
    def _():
        issue_chunk(f + 1)
        x = xb[...]
        a = jnp.dot(x, wg_ref[0], preferred_element_type=F32)
        b = jnp.dot(x, wu_ref[0], preferred_element_type=F32)
        acc[slot] += jnp.dot((_silu(a) * b).astype(BF16), wd_ref[0], preferred_element_type=F32)


def _moe(h, wg, wu, wd, tile_expert, n_active, route, n_out_rows, tm=MOE_TM):
    d = h.shape[1]
    ff = wg.shape[2]
    n_tiles = tile_expert.shape[0]
    nf = MOE_NF
    tf = ff // nf

    def fidx(i, f, nact):
        return jnp.where(i < nact[0], f, nf - 1)

    grid_spec = pltpu.PrefetchScalarGridSpec(
        num_scalar_prefetch=3,
        grid=(n_tiles, nf),
        in_specs=[pl.BlockSpec(memory_space=pl.ANY),
                  pl.BlockSpec((1, d, tf), lambda i, f, te, nact, route: (te[i], 0, fidx(i, f, nact))),
                  pl.BlockSpec((1, d, tf), lambda i, f, te, nact, route: (te[i], 0, fidx(i, f, nact))),
                  pl.BlockSpec((1, tf, d), lambda i, f, te, nact, route: (te[i], fidx(i, f, nact), 0))],
        out_specs=pl.BlockSpec(memory_space=pl.ANY),
        scratch_shapes=[pltpu.VMEM((tm, d), F32), pltpu.VMEM((tm, d), BF16), pltpu.VMEM((2, tm, d), F32),
                        pltpu.SemaphoreType.DMA, pltpu.SemaphoreType.DMA],
    )
    return pl.pallas_call(
        _moe_kernel,
        grid_spec=grid_spec,
        out_shape=jax.ShapeDtypeStruct((n_out_rows, d), F32),
        compiler_params=_params(("arbitrary", "arbitrary")),
        name="moe",
    )(tile_expert, n_active, route, h, wg, wu, wd)


def _route_plan(r, n_tok, tm):
    n_assign = 2 * n_tok
    n_tiles = n_assign // tm + N_EXPERTS
    e_flat = jnp.concatenate([r[:, 0], r[:, 1]]).astype(jnp.int32)
    experts = jnp.arange(N_EXPERTS, dtype=jnp.int32)
    counts = jnp.sum(e_flat[:, None] == experts[None, :], axis=0).astype(jnp.int32)
    tiles_per = (counts + tm - 1) // tm
    tend = jnp.cumsum(tiles_per)
    n_active = tend[-1]
    j = jnp.arange(n_tiles, dtype=jnp.int32)
    te = jnp.minimum(jnp.sum(j[:, None] >= tend[None, :], axis=1), N_EXPERTS - 1).astype(jnp.int32)
    last = jnp.sum(jnp.where(j == n_active - 1, te, 0))
    te = jnp.where(j < n_active, te, last)
    pad_id = jnp.arange(tm, dtype=jnp.int32)
    pad_on = pad_id[None, :] < (tiles_per * tm - counts)[:, None]
    pad_key = jnp.where(pad_on, 2 * experts[:, None] + 1, 2 * N_EXPERTS).reshape(-1)
    keys = jnp.concatenate([2 * e_flat, pad_key])
    item = jnp.argsort(keys, stable=True).astype(jnp.int32)
    row = jnp.arange(n_tiles * tm, dtype=jnp.int32)
    real = item < n_assign
    src = jnp.where(real, item % n_tok, 0)
    dst = jnp.where(real, item, n_assign + row % tm)
    body = src | (dst << SRC_BITS)
    edge = (jnp.arange(tm, dtype=jnp.int32) + n_assign) << SRC_BITS
    route = jnp.concatenate([edge, body, edge]).astype(jnp.int32)
    return te, n_active.reshape(1).astype(jnp.int32), route


def _combine_kernel(x_ref, y0_ref, y1_ref, r_ref, mod_ref, g_ref, o_ref):
    r = r_ref[...]
    y = r[:, 2:3] * y0_ref[...] + r[:, 3:4] * y1_ref[...]
    x = x_ref[...] + mod_ref[0, 5:6, :] * y
    ms = jnp.mean(x * x, axis=-1, keepdims=True)
    o_ref[...] = x * lax.rsqrt(ms + EPS) * g_ref[...]


def _combine(xs, y, r, mods, mod_index, g, tm=MM_TILE):
    t, d = xs.shape
    nb = t // tm
    return pl.pallas_call(
        _combine_kernel,
        grid=(nb,),
        in_specs=[pl.BlockSpec((tm, d), lambda i: (i, 0)),
                  pl.BlockSpec((tm, d), lambda i: (i, 0)),
                  pl.BlockSpec((tm, d), lambda i: (nb + i, 0)),
                  pl.BlockSpec((tm, LANES), lambda i: (i, 0)),
                  pl.BlockSpec((1, N_MOD, d), mod_index),
                  _resident((1, d))],
        out_specs=pl.BlockSpec((tm, d), lambda i: (i, 0)),
        out_shape=jax.ShapeDtypeStruct((t, d), F32),
        compiler_params=_params(("arbitrary",)),
        name="combine",
    )(xs, y, y, r, mods, g.reshape(1, d))


def _rope_tables(seq, extra_rows):
    rows = seq // GRID_W
    row = np.repeat(np.arange(rows), GRID_W).astype(np.float32)
    col = np.tile(np.arange(GRID_W), rows).astype(np.float32)
    inv = np.float32(ROPE_THETA) ** (-np.arange(ROPE_FREQS, dtype=np.float32) / np.float32(ROPE_FREQS))
    ar = row[:, None] * inv
    ac = col[:, None] * inv
    cos = np.ones((seq + extra_rows, LANES), np.float32)
    sin = np.zeros((seq + extra_rows, LANES), np.float32)
    cos[:seq, :QK_ROPE] = np.concatenate([np.cos(ar), np.cos(ar), np.cos(ac), np.cos(ac)], axis=1)
    sin[:seq, :QK_ROPE] = np.concatenate([-np.sin(ar), np.sin(ar), -np.sin(ac), np.sin(ac)], axis=1)
    return jnp.asarray(cos), jnp.asarray(sin)


def kernel(x, c, ctx, c_ctx, ada_w, ada_b, norm1_g, norm2_g, e_w_in, e_pool_w, e_pool_scale, e_conv_w, e_conv_b, e_conv_ln_g, e_conv_ln_b, e_w_out, e_ffn_w_gate, e_ffn_w_up, e_ffn_w_down, o_w_in, o_q_norm_g, o_w_qb, o_kv_norm_g, o_w_kvb, o_sgu_ln_g, o_sgu_ln_b, o_sgu_w, o_sgu_b, o_w_out, o_router_w, o_router_b, o_exp_w_gate, o_exp_w_up, o_exp_w_down, final_norm_g):
    n_batch, seq, d = x.shape
    ctx_len = ctx.shape[1]
    n_lat = n_batch * seq
    n_ctx = n_batch * ctx_len
    assert ctx_len % ROW_TILE == 0 and seq % MM_TILE == 0 and n_ctx % MM_TILE == 0
    assert ada_w.shape[0] == 2 and e_w_in.shape[0] == 1 and o_w_in.shape[0] == 1

    rows_per_layer = 8 * ((n_batch + 1 + 7) // 8)
    cond = jnp.zeros((rows_per_layer, d), F32).at[:n_batch].set(c).at[n_batch].set(c_ctx)
    mods = _ada(cond, ada_w, ada_b).reshape(2 * rows_per_layer, N_MOD, d)

    def mod_index(layer, tm):
        return _mod_spec(layer, tm, n_lat, seq, n_batch, rows_per_layer)

    x_lat = x.reshape(n_lat, d)
    x_ctx = ctx.reshape(n_ctx, d)

    p, (fg, fu, fd) = _inproj(x_lat, x_ctx, mods, mod_index(0, MM_TILE), norm1_g[0], e_w_in[0].astype(BF16),
                              [e_ffn_w_gate[0], e_ffn_w_up[0], e_ffn_w_down[0]])
    y = _poolconv(p, e_pool_w[0].astype(BF16), e_pool_scale[0], e_conv_w[0], e_conv_b[0],
                  e_conv_ln_g[0], e_conv_ln_b[0], n_lat, seq, ctx_len)
    x1, h2 = _outproj([y], e_w_out[0].astype(BF16), (x_lat, x_ctx), mods, mod_index(0, MM_TILE), norm2_g[0],
                      n_lat + n_ctx, BF16)
    x2 = _ffn(h2, fg, fu, fd, x1, mods, mod_index(0, MM_TILE))

    w_in = o_w_in[0]
    c_kr = Q_LORA + KV_LORA
    c_u = c_kr + QK_ROPE
    w_lora = w_in[:, :c_kr].astype(BF16)
    w_kr = jnp.pad(w_in[:, c_kr:c_u], ((0, 0), (0, LANES - QK_ROPE))).astype(BF16)
    w_sgu = w_in[:, c_u:].astype(BF16)
    wq = o_w_qb[0].reshape(Q_LORA, MLA_HEADS, QK_NOPE + QK_ROPE)
    wq = jnp.pad(wq, ((0, 0), (0, 0), (0, HEAD_PAD - QK_NOPE - QK_ROPE)))
    wq = wq.reshape(Q_LORA, MLA_HEADS * HEAD_PAD).astype(BF16)
    wkv = o_w_kvb[0].reshape(KV_LORA, MLA_HEADS, QK_NOPE + V_HEAD)
    wkv = jnp.concatenate([wkv[:, :, :QK_NOPE].reshape(KV_LORA, -1),
                           wkv[:, :, QK_NOPE:].reshape(KV_LORA, -1)], axis=1).astype(BF16)
    cos_t, sin_t = _rope_tables(seq, MM_TILE)
    d_sgu = o_sgu_ln_g.shape[1]
    sgu_bias = jnp.repeat(o_sgu_b[0].T, d_sgu // SGU_HEADS, axis=1)
    q, k, v, sg = _mla_proj(x2, mods, mod_index(1, MM_TILE), norm1_g[1], w_lora, w_kr, w_sgu,
                            o_q_norm_g[0], wq, o_kv_norm_g[0], wkv, cos_t, sin_t,
                            o_sgu_ln_g[0], o_sgu_ln_b[0], o_sgu_w[0].astype(BF16), sgu_bias,
                            n_lat, seq)
    experts = (o_exp_w_gate[0], o_exp_w_up[0], o_exp_w_down[0])
    attn, experts_bf16 = _attention(q, k, v, n_batch, seq, ctx_len,
                                    [w.reshape(-1, w.shape[-1]) for w in experts])
    wg, wu, wd = (wb.reshape(w.shape) for wb, w in zip(experts_bf16, experts))

    rw = jnp.pad(o_router_w[0], ((0, 0), (0, LANES - N_EXPERTS))).astype(BF16)
    rb = jnp.concatenate([o_router_b[0], jnp.full((LANES - N_EXPERTS,), NEG, F32)]).reshape(1, LANES)
    x3, h3, r = _outproj([attn, sg], o_w_out[0].astype(BF16), (x2,), mods, mod_index(1, MM_TILE),
                         norm2_g[1], n_lat, F32, router=(rw, rb))

    te, n_active, route = _route_plan(r, n_lat, MOE_TM)
    ys = _moe(h3, wg, wu, wd, te, n_active, route, 2 * n_lat + MOE_TM)
    out = _combine(x3, ys, r, mods, mod_index(1, MM_TILE), final_norm_g)
    return out.reshape(n_batch, seq, d)
```

```python
import functools

import jax
import jax.numpy as jnp
import numpy as np
from jax import lax
from jax.experimental import pallas as pl
from jax.experimental.pallas import tpu as pltpu

F32 = jnp.float32
BF16 = jnp.bfloat16

EPS = 1e-6
N_MOD = 6
GRID_W = 64
POOL_WINDOWS = (2, 4, 8, 16)
CONV_WIDTH = 31
MLA_HEADS = 8
Q_LORA = 512
KV_LORA = 512
QK_NOPE = 128
QK_ROPE = 64
V_HEAD = 128
ROPE_FREQS = QK_ROPE // 4
ROPE_THETA = 10000.0
SM_SCALE = (QK_NOPE + QK_ROPE) ** -0.5
LOG2E = 1.4426950408889634
SGU_HEADS = 8
CHUNK = 128
N_EXPERTS = 8

LANES = 128
SUBLANES = 8
BF16_ROWS = 16
HALO = 16
VMEM_LIMIT = 56 * 1024 * 1024
HEAD_PAD = 256
ROW_TILE = 256
MM_TILE = 512
FFN_TF = 512
MOE_TM = 512
MOE_NF = 7
ATTN_HEADS_PER_STEP = 2
NEG = -1e30


def _params(sem):
    return pltpu.CompilerParams(dimension_semantics=sem, vmem_limit_bytes=VMEM_LIMIT)


def _resident(shape):
    nd = len(shape)
    return pl.BlockSpec(shape, lambda *_: (0,) * nd, pipeline_mode=pl.Buffered(1))


def _slab_specs(arrays, n_steps, step_of):
    specs = []
    for w in arrays:
        rows, cols = w.shape
        n = max(k for k in range(1, n_steps + 1) if rows % (BF16_ROWS * k) == 0)
        specs.append(pl.BlockSpec((rows // n, cols),
                                  lambda *idx, n=n: (jnp.minimum(step_of(*idx), n - 1), 0)))
    return specs


def _cast_slabs(in_refs, out_refs):
    for w_ref, wb_ref in zip(in_refs, out_refs):
        wb_ref[...] = w_ref[...].astype(BF16)


def _rmsnorm_mod(x, g, shift, scale):
    ms = jnp.mean(x * x, axis=-1, keepdims=True)
    return (x * lax.rsqrt(ms + EPS) * g) * (1.0 + scale) + shift


def _layernorm(x, g, b):
    mu = jnp.mean(x, axis=-1, keepdims=True)
    xc = x - mu
    var = jnp.mean(xc * xc, axis=-1, keepdims=True)
    return xc * lax.rsqrt(var + EPS) * g + b


def _silu(x):
    return x * jax.nn.sigmoid(x)


def _ada_kernel(c_ref, w_ref, b_ref, o_ref):
    s = _silu(c_ref[...]).astype(BF16)
    o_ref[0] = jnp.dot(s, w_ref[0].astype(BF16), preferred_element_type=F32) + b_ref[0]


def _ada(cond, ada_w, ada_b, tn=1024):
    depth, d, n = ada_w.shape
    rows = cond.shape[0]
    return pl.pallas_call(
        _ada_kernel,
        grid=(depth, n // tn),
        in_specs=[pl.BlockSpec((rows, d), lambda l, j: (0, 0)),
                  pl.BlockSpec((1, d, tn), lambda l, j: (l, 0, j)),
                  pl.BlockSpec((1, 1, tn), lambda l, j: (l, 0, j))],
        out_specs=pl.BlockSpec((1, rows, tn), lambda l, j: (l, 0, j)),
        out_shape=jax.ShapeDtypeStruct((depth, rows, n), F32),
        compiler_params=_params(("arbitrary", "arbitrary")),
        name="ada",
    )(cond, ada_w, ada_b.reshape(depth, 1, n))


def _mod_spec(layer, tm, n_lat_rows, seq, n_batch, rows_per_layer):
    def index(i, *_):
        r = jnp.where(i * tm < n_lat_rows, (i * tm) // seq, n_batch)
        return (layer * rows_per_layer + r, 0, 0)
    return index


def _pick_rows(i, n_lat_tiles, lat_ref, ctx_ref):
    return jnp.where(i < n_lat_tiles, lat_ref[...], ctx_ref[...])


def _two_source_specs(tm, d, n_lat_tiles):
    return [pl.BlockSpec((tm, d), lambda i: (jnp.minimum(i, n_lat_tiles - 1), 0)),
            pl.BlockSpec((tm, d), lambda i: (jnp.maximum(i - n_lat_tiles, 0), 0))]


def _inproj_kernel(*refs, n_lat_tiles, n_cast):
    xl_ref, xc_ref, mod_ref, g_ref, w_ref = refs[:5]
    o_ref = refs[5 + n_cast]
    x = _pick_rows(pl.program_id(0), n_lat_tiles, xl_ref, xc_ref)
    h = _rmsnorm_mod(x, g_ref[...], mod_ref[0, 0:1, :], mod_ref[0, 1:2, :])
    o_ref[...] = jnp.dot(h.astype(BF16), w_ref[...], preferred_element_type=F32).astype(o_ref.dtype)
    _cast_slabs(refs[5:5 + n_cast], refs[6 + n_cast:])


def _inproj(x_lat, x_ctx, mods, mod_index, g, w, casts, tm=MM_TILE):
    d = x_lat.shape[1]
    t = x_lat.shape[0] + x_ctx.shape[0]
    n = w.shape[1]
    n_lat_tiles = x_lat.shape[0] // tm
    cast_specs = _slab_specs(casts, t // tm, lambda i: i)
    outs = pl.pallas_call(
        functools.partial(_inproj_kernel, n_lat_tiles=n_lat_tiles, n_cast=len(casts)),
        grid=(t // tm,),
        in_specs=_two_source_specs(tm, d, n_lat_tiles) + [
            pl.BlockSpec((1, N_MOD, d), mod_index), _resident((1, d)), _resident((d, n))] + cast_specs,
        out_specs=[pl.BlockSpec((tm, n), lambda i: (i, 0))] + cast_specs,
        out_shape=[jax.ShapeDtypeStruct((t, n), BF16)] + [jax.ShapeDtypeStruct(c.shape, BF16) for c in casts],
        compiler_params=_params(("arbitrary",)),
        name="inproj0",
    )(x_lat, x_ctx, mods, g.reshape(1, d), w, *casts)
    return outs[0], outs[1:]


def _poolconv_kernel(up_ref, um_ref, un_ref, ap_ref, am_ref, an_ref, gp_ref, gm_ref, gn_ref,
                     pw_ref, ps_ref, cw_ref, cb_ref, lg_ref, lb_ref, y_ref,
                     ubuf, zbuf, zsh, cbuf, *, n_lat_tiles, seq, ctx_len):
    i = pl.program_id(0)
    tm = um_ref.shape[0]
    dp = um_ref.shape[1]
    is_lat = i < n_lat_tiles
    pos0 = jnp.where(is_lat, (i * tm) % seq, ((i - n_lat_tiles) * tm) % ctx_len)
    length = jnp.where(is_lat, seq, ctx_len)
    keep_p = jnp.where(pos0 == 0, 0.0, 1.0).astype(F32)
    keep_n = jnp.where(pos0 + tm == length, 0.0, 1.0).astype(F32)

    def glu(a_ref, g_ref):
        return a_ref[...].astype(F32) * jax.nn.sigmoid(g_ref[...].astype(F32))

    ubuf[0:HALO, :] = up_ref[...].astype(F32) * keep_p
    ubuf[HALO:HALO + tm, :] = um_ref[...].astype(F32)
    ubuf[HALO + tm:, :] = un_ref[...].astype(F32) * keep_n
    zbuf[0:HALO, :] = glu(ap_ref, gp_ref) * keep_p
    zbuf[HALO:HALO + tm, :] = glu(am_ref, gm_ref)
    zbuf[HALO + tm:, :] = glu(an_ref, gn_ref) * keep_n

    pos = pos0 + lax.broadcasted_iota(jnp.int32, (tm, 1), 0)
    gdim = dp // len(POOL_WINDOWS)
    for g, w in enumerate(POOL_WINDOWS):
        cols = slice(g * gdim, (g + 1) * gdim)
        s = ubuf[HALO - w // 2:HALO - w // 2 + tm, cols]
        for o in range(-w // 2 + 1, w // 2):
            s = s + ubuf[HALO + o:HALO + o + tm, cols]
        cnt = jnp.minimum(pos - w // 2 + w, length) - jnp.maximum(pos - w // 2, 0)
        pooled = s / cnt.astype(F32) - ubuf[HALO:HALO + tm, cols]
        mixed = jnp.dot(pooled.astype(BF16), pw_ref[g], preferred_element_type=F32)
        y_ref[:, cols] = (mixed * ps_ref[:, cols]).astype(y_ref.dtype)

    n_sh = zsh.shape[1]
    for j in range(1, SUBLANES):
        zsh[j - 1] = zbuf[j:j + n_sh, :]
    rb = 64
    first = HALO - CONV_WIDTH // 2
    for c in range(0, zbuf.shape[1], LANES):
        cols = slice(c, c + LANES)
        taps = [cw_ref[k:k + 1, cols] for k in range(CONV_WIDTH)]
        for r in range(0, tm, rb):
            acc = None
            for k in range(CONV_WIDTH):
                q, j = divmod(first + k, SUBLANES)
                rows = slice(q * SUBLANES + r, q * SUBLANES + r + rb)
                z = zbuf[rows, cols] if j == 0 else zsh[j - 1, rows, cols]
                acc = taps[k] * z if acc is None else acc + taps[k] * z
            cbuf[r:r + rb, cols] = acc
    conv = _layernorm(cbuf[...] + cb_ref[...], lg_ref[...], lb_ref[...])
    y_ref[:, dp:] = _silu(conv).astype(y_ref.dtype)


def _poolconv(p, pool_w, pool_scale, conv_w, conv_b, ln_g, ln_b, n_lat_rows, seq, ctx_len,
              tm=ROW_TILE):
    t = p.shape[0]
    dp = pool_scale.shape[0]
    dc = conv_b.shape[0]
    hb = tm // HALO
    last_hb = t // HALO - 1

    def main(c):
        return pl.BlockSpec((tm, dp), lambda i: (i, c))

    def prev(c):
        return pl.BlockSpec((HALO, dp), lambda i: (jnp.maximum(i * hb - 1, 0), c))

    def nxt(c):
        return pl.BlockSpec((HALO, dp), lambda i: (jnp.minimum((i + 1) * hb, last_hb), c))

    kern = functools.partial(_poolconv_kernel, n_lat_tiles=n_lat_rows // tm, seq=seq, ctx_len=ctx_len)
    return pl.pallas_call(
        kern,
        grid=(t // tm,),
        in_specs=[prev(0), main(0), nxt(0), prev(1), main(1), nxt(1), prev(2), main(2), nxt(2),
                  _resident(pool_w.shape), _resident((1, dp)), _resident(conv_w.shape),
                  _resident((1, dc)), _resident((1, dc)), _resident((1, dc))],
        out_specs=pl.BlockSpec((tm, dp + dc), lambda i: (i, 0)),
        out_shape=jax.ShapeDtypeStruct((t, dp + dc), BF16),
        scratch_shapes=[pltpu.VMEM((tm + 2 * HALO, dp), F32),
                        pltpu.VMEM((tm + 2 * HALO, dc), F32),
                        pltpu.VMEM((SUBLANES - 1, tm + 2 * HALO - SUBLANES, dc), F32),
                        pltpu.VMEM((tm, dc), F32)],
        compiler_params=_params(("arbitrary",)),
        name="poolconv",
    )(p, p, p, p, p, p, p, p, p, pool_w, pool_scale.reshape(1, dp), conv_w,
      conv_b.reshape(1, dc), ln_g.reshape(1, dc), ln_b.reshape(1, dc))


def _outproj_kernel(*refs, n_in, n_res, n_lat_tiles, router):
    y_refs = refs[:n_in]
    w_ref = refs[n_in]
    x_refs = refs[n_in + 1:n_in + 1 + n_res]
    mod_ref, g_ref = refs[n_in + 1 + n_res:n_in + 3 + n_res]
    rest = refs[n_in + 3 + n_res:]
    if router:
        rw_ref, rb_ref, xo_ref, ho_ref, r_ref = rest
    else:
        xo_ref, ho_ref = rest
    o = None
    k0 = 0
    for y_ref in y_refs:
        kk = y_ref.shape[1]
        part = jnp.dot(y_ref[...], w_ref[k0:k0 + kk, :], preferred_element_type=F32)
        o = part if o is None else o + part
        k0 += kk
    if n_res == 2:
        x = _pick_rows(pl.program_id(0), n_lat_tiles, x_refs[0], x_refs[1])
    else:
        x = x_refs[0][...]
    x1 = x + mod_ref[0, 2:3, :] * o
    xo_ref[...] = x1
    h = _rmsnorm_mod(x1, g_ref[...], mod_ref[0, 3:4, :], mod_ref[0, 4:5, :])
    ho_ref[...] = h.astype(ho_ref.dtype)
    if router:
        logits = jnp.dot(h.astype(BF16), rw_ref[...], preferred_element_type=F32) + rb_ref[...]
        lane = lax.broadcasted_iota(jnp.int32, logits.shape, 1)
        m1 = jnp.max(logits, axis=-1, keepdims=True)
        i1 = jnp.min(jnp.where(logits == m1, lane, LANES), axis=-1, keepdims=True)
        rest_l = jnp.where(lane == i1, NEG * 2, logits)
        m2 = jnp.max(rest_l, axis=-1, keepdims=True)
        i2 = jnp.min(jnp.where(rest_l == m2, lane, LANES), axis=-1, keepdims=True)
        e2 = jnp.exp(m2 - m1)
        p1 = 1.0 / (1.0 + e2)
        p2 = e2 / (1.0 + e2)
        r = jnp.where(lane == 0, i1.astype(F32), 0.0)
        r = jnp.where(lane == 1, i2.astype(F32), r)
        r = jnp.where(lane == 2, p1, r)
        r = jnp.where(lane == 3, p2, r)
        r_ref[...] = r


def _outproj(ys, w, res, mods, mod_index, g, n_rows, h_dtype, router=None, tm=MM_TILE):
    d = w.shape[1]
    in_specs = [pl.BlockSpec((tm, y.shape[1]), lambda i: (i, 0)) for y in ys]
    in_specs.append(_resident(w.shape))
    n_lat_tiles = res[0].shape[0] // tm
    if len(res) == 2:
        in_specs += _two_source_specs(tm, d, n_lat_tiles)
    else:
        in_specs.append(pl.BlockSpec((tm, d), lambda i: (i, 0)))
    in_specs += [pl.BlockSpec((1, N_MOD, d), mod_index), _resident((1, d))]
    args = list(ys) + [w] + list(res) + [mods, g.reshape(1, d)]
    out_specs = [pl.BlockSpec((tm, d), lambda i: (i, 0)), pl.BlockSpec((tm, d), lambda i: (i, 0))]
    out_shape = [jax.ShapeDtypeStruct((n_rows, d), F32), jax.ShapeDtypeStruct((n_rows, d), h_dtype)]
    if router is not None:
        rw, rb = router
        in_specs += [_resident(rw.shape), _resident(rb.shape)]
        args += [rw, rb]
        out_specs.append(pl.BlockSpec((tm, LANES), lambda i: (i, 0)))
        out_shape.append(jax.ShapeDtypeStruct((n_rows, LANES), F32))
    kern = functools.partial(_outproj_kernel, n_in=len(ys), n_res=len(res), n_lat_tiles=n_lat_tiles,
                             router=router is not None)
    return pl.pallas_call(
        kern,
        grid=(n_rows // tm,),
        in_specs=in_specs,
        out_specs=out_specs,
        out_shape=out_shape,
        compiler_params=_params(("arbitrary",)),
        name="outproj_router" if router is not None else "outproj",
    )(*args)


def _ffn_kernel(h_ref, wg_ref, wu_ref, wd_ref, x_ref, mod_ref, o_ref, acc_ref):
    f = pl.program_id(1)

    @pl.when(f == 0)
    def _():
        acc_ref[...] = jnp.zeros(acc_ref.shape, F32)

    h = h_ref[...]
    a = jnp.dot(h, wg_ref[...], preferred_element_type=F32)
    b = jnp.dot(h, wu_ref[...], preferred_element_type=F32)
    acc_ref[...] += jnp.dot((_silu(a) * b).astype(BF16), wd_ref[...], preferred_element_type=F32)

    @pl.when(f == pl.num_programs(1) - 1)
    def _():
        o_ref[...] = x_ref[...] + mod_ref[0, 5:6, :] * acc_ref[...]


def _ffn(h, wg, wu, wd, xs, mods, mod_index, tm=MM_TILE, tf=FFN_TF):
    t, d = xs.shape
    ff = wg.shape[1]
    return pl.pallas_call(
        _ffn_kernel,
        grid=(t // tm, ff // tf),
        in_specs=[pl.BlockSpec((tm, d), lambda i, f: (i, 0)),
                  pl.BlockSpec((d, tf), lambda i, f: (0, f)),
                  pl.BlockSpec((d, tf), lambda i, f: (0, f)),
                  pl.BlockSpec((tf, d), lambda i, f: (f, 0)),
                  pl.BlockSpec((tm, d), lambda i, f: (i, 0)),
                  pl.BlockSpec((1, N_MOD, d), mod_index)],
        out_specs=pl.BlockSpec((tm, d), lambda i, f: (i, 0)),
        out_shape=jax.ShapeDtypeStruct((t, d), F32),
        scratch_shapes=[pltpu.VMEM((tm, d), F32)],
        compiler_params=_params(("arbitrary", "arbitrary")),
        name="ffn",
    )(h, wg, wu, wd, xs, mods)


def _rope(x, c, s):
    lane = lax.broadcasted_iota(jnp.int32, x.shape, 1)
    partner = jnp.where(lane % 32 < 16, pltpu.roll(x, LANES - 16, 1), pltpu.roll(x, 16, 1))
    return x * c + partner * s


def _mla_kernel(x_ref, mod_ref, g_ref, wlora_ref, wkr_ref, wsgu_ref, qg_ref, wq_ref, kvg_ref, wkv_ref,
                cos_ref, sin_ref, slg_ref, slb_ref, sw_ref, sb_ref,
                q_ref, k_ref, v_ref, sg_ref, *, n_lat_tiles):
    i = pl.program_id(0)
    tm = x_ref.shape[0]
    h = _rmsnorm_mod(x_ref[...], g_ref[...], mod_ref[0, 0:1, :], mod_ref[0, 1:2, :]).astype(BF16)
    p_lora = jnp.dot(h, wlora_ref[...], preferred_element_type=F32)
    p_sgu = jnp.dot(h, wsgu_ref[...], preferred_element_type=F32)
    d_sgu = slg_ref.shape[1]
    cos = cos_ref[...]
    sin = sin_ref[...]

    def rms(x, g):
        return x * lax.rsqrt(jnp.mean(x * x, axis=-1, keepdims=True) + EPS) * g

    ckv = rms(p_lora[:, Q_LORA:], kvg_ref[...]).astype(BF16)
    kv = jnp.dot(ckv, wkv_ref[...], preferred_element_type=F32)
    kr = _rope(jnp.dot(h, wkr_ref[...], preferred_element_type=F32), cos, sin).astype(BF16)
    nk = MLA_HEADS * QK_NOPE
    for hd in range(MLA_HEADS):
        k_ref[:, hd * HEAD_PAD:hd * HEAD_PAD + QK_NOPE] = kv[:, hd * QK_NOPE:(hd + 1) * QK_NOPE].astype(BF16)
        k_ref[:, hd * HEAD_PAD + QK_NOPE:(hd + 1) * HEAD_PAD] = kr
        v_ref[:, hd * HEAD_PAD:hd * HEAD_PAD + V_HEAD] = kv[:, nk + hd * V_HEAD:nk + (hd + 1) * V_HEAD].astype(BF16)
        v_ref[:, hd * HEAD_PAD + V_HEAD:(hd + 1) * HEAD_PAD] = jnp.ones((tm, HEAD_PAD - V_HEAD), BF16)

    @pl.when(i < n_lat_tiles)
    def _():
        cq = rms(p_lora[:, :Q_LORA], qg_ref[...]).astype(BF16)
        q = jnp.dot(cq, wq_ref[...], preferred_element_type=F32) * (SM_SCALE * LOG2E)
        for hd in range(MLA_HEADS):
            b0 = hd * HEAD_PAD
            q_ref[:, b0:b0 + QK_NOPE] = q[:, b0:b0 + QK_NOPE].astype(BF16)
            q_ref[:, b0 + QK_NOPE:b0 + HEAD_PAD] = _rope(q[:, b0 + QK_NOPE:b0 + HEAD_PAD], cos, sin).astype(BF16)

        zu = jax.nn.gelu(p_sgu[:, :d_sgu])
        zv = _layernorm(jax.nn.gelu(p_sgu[:, d_sgu:]), slg_ref[...], slb_ref[...]).astype(BF16)
        hdim = d_sgu // SGU_HEADS
        for ck in range(tm // CHUNK):
            rows = slice(ck * CHUNK, (ck + 1) * CHUNK)
            for hd in range(SGU_HEADS):
                cols = slice(hd * hdim, (hd + 1) * hdim)
                mixed = jnp.dot(sw_ref[hd], zv[rows, cols], preferred_element_type=F32) + sb_ref[:, cols]
                sg_ref[rows, cols] = (zu[rows, cols] * mixed).astype(BF16)


def _mla_proj(xs, mods, mod_index, g, w_lora, w_kr, w_sgu, qg, wq, kvg, wkv, cos_t, sin_t, slg, slb, sw, sb,
              n_lat_rows, seq, tm=MM_TILE):
    t, d = xs.shape
    n_lat_tiles = n_lat_rows // tm
    seq_tiles = seq // tm
    d_sgu = slg.shape[0]

    def lat(i):
        return (jnp.minimum(i, n_lat_tiles - 1), 0)

    def table(i):
        return (jnp.where(i < n_lat_tiles, i % seq_tiles, seq_tiles), 0)

    kern = functools.partial(_mla_kernel, n_lat_tiles=n_lat_tiles)
    return pl.pallas_call(
        kern,
        grid=(t // tm,),
        in_specs=[pl.BlockSpec((tm, d), lambda i: (i, 0)),
                  pl.BlockSpec((1, N_MOD, d), mod_index),
                  _resident((1, d)), _resident(w_lora.shape), _resident(w_kr.shape), _resident(w_sgu.shape),
                  _resident((1, Q_LORA)), _resident(wq.shape),
                  _resident((1, KV_LORA)), _resident(wkv.shape),
                  pl.BlockSpec((tm, LANES), table), pl.BlockSpec((tm, LANES), table),
                  _resident((1, d_sgu)), _resident((1, d_sgu)), _resident(sw.shape), _resident(sb.shape)],
        out_specs=[pl.BlockSpec((tm, MLA_HEADS * HEAD_PAD), lat),
                   pl.BlockSpec((tm, MLA_HEADS * HEAD_PAD), lambda i: (i, 0)),
                   pl.BlockSpec((tm, MLA_HEADS * HEAD_PAD), lambda i: (i, 0)),
                   pl.BlockSpec((tm, d_sgu), lat)],
        out_shape=[jax.ShapeDtypeStruct((n_lat_rows, MLA_HEADS * HEAD_PAD), BF16),
                   jax.ShapeDtypeStruct((t, MLA_HEADS * HEAD_PAD), BF16),
                   jax.ShapeDtypeStruct((t, MLA_HEADS * HEAD_PAD), BF16),
                   jax.ShapeDtypeStruct((n_lat_rows, d_sgu), BF16)],
        compiler_params=_params(("arbitrary",)),
        name="mla_proj",
    )(xs, mods, g.reshape(1, d), w_lora, w_kr, w_sgu, qg.reshape(1, -1), wq, kvg.reshape(1, -1), wkv,
      cos_t, sin_t, slg.reshape(1, -1), slb.reshape(1, -1), sw, sb)


def _attn_kernel(*refs, n_cast):
    q_ref, kc_ref, kl_ref, vc_ref, vl_ref = refs[:5]
    cast_in = refs[5:5 + n_cast]
    o_ref = refs[5 + n_cast]
    cast_out = refs[6 + n_cast:]
    dn = (((1,), (1,)), ((), ()))

    def scores(hd):
        qk = slice(hd * HEAD_PAD, (hd + 1) * HEAD_PAD)
        q = q_ref[:, qk]
        return (lax.dot_general(q, kc_ref[:, qk], dn, preferred_element_type=F32),
                lax.dot_general(q, kl_ref[:, qk], dn, preferred_element_type=F32))

    def finish(hd, sc, sl):
        vv = slice(hd * HEAD_PAD, (hd + 1) * HEAD_PAD)
        m = jnp.maximum(jnp.max(sc, axis=-1, keepdims=True), jnp.max(sl, axis=-1, keepdims=True))
        pc = jnp.exp2((sc - m).astype(BF16))
        pq = jnp.exp2((sl - m).astype(BF16))
        o = (jnp.dot(pc, vc_ref[:, vv], preferred_element_type=F32)
             + jnp.dot(pq, vl_ref[:, vv], preferred_element_type=F32))
        out = slice(hd * V_HEAD, (hd + 1) * V_HEAD)
        o_ref[:, out] = (o[:, :V_HEAD] / o[:, V_HEAD:V_HEAD + 1]).astype(o_ref.dtype)

    s_next = scores(0)
    for hd in range(ATTN_HEADS_PER_STEP):
        s_cur = s_next
        if hd + 1 < ATTN_HEADS_PER_STEP:
            s_next = scores(hd + 1)
        finish(hd, *s_cur)
    _cast_slabs(cast_in, cast_out)


def _attention(q, k, v, n_batch, seq, ctx_len, casts, tq=ROW_TILE):
    n_lat_rows = n_batch * seq
    qb = seq // tq
    ctx_blk0 = n_lat_rows // ctx_len
    hs = ATTN_HEADS_PER_STEP
    hp = MLA_HEADS // hs
    n_steps = n_batch * hp * qb

    cast_specs = _slab_specs(casts, n_steps, lambda b, h, j: (b * hp + h) * qb + j)
    outs = pl.pallas_call(
        functools.partial(_attn_kernel, n_cast=len(casts)),
        grid=(n_batch, hp, qb),
        in_specs=[pl.BlockSpec((tq, hs * HEAD_PAD), lambda b, h, j: (b * qb + j, h)),
                  pl.BlockSpec((ctx_len, hs * HEAD_PAD), lambda b, h, j: (ctx_blk0 + b, h)),
                  pl.BlockSpec((seq, hs * HEAD_PAD), lambda b, h, j: (b, h)),
                  pl.BlockSpec((ctx_len, hs * HEAD_PAD), lambda b, h, j: (ctx_blk0 + b, h)),
                  pl.BlockSpec((seq, hs * HEAD_PAD), lambda b, h, j: (b, h))] + cast_specs,
        out_specs=[pl.BlockSpec((tq, hs * V_HEAD), lambda b, h, j: (b * qb + j, h))] + cast_specs,
        out_shape=[jax.ShapeDtypeStruct((n_lat_rows, MLA_HEADS * V_HEAD), BF16)]
        + [jax.ShapeDtypeStruct(w.shape, BF16) for w in casts],
        compiler_params=_params(("arbitrary", "arbitrary", "arbitrary")),
        name="attention",
    )(q, k, k, v, v, *casts)
    return outs[0], outs[1:]


SRC_BITS = 15
ITEM_RANGE = 1 << 16


def _moe_kernel(te_ref, nact_ref, rows_ref, route_ref, h_hbm, wg_ref, wu_ref, wd_ref, y_hbm,
                xbuf, xb, acc, gsem, ssem):
    i = pl.program_id(0)
    f = pl.program_id(1)
    tm = xbuf.shape[0]
    chunk = tm // (MOE_NF + 1)
    nact = nact_ref[0]
    active = i < nact
    slot = i % 2
    dump0 = y_hbm.shape[0] - tm

    def gather_copy(tile, r):
        src = route_ref[(tile + 1) * tm + r] & ((1 << SRC_BITS) - 1)
        return pltpu.make_async_copy(h_hbm.at[pl.ds(src, 1)], xbuf.at[pl.ds(r, 1)], gsem)

    def scatter_copy(tile, r):
        dst = route_ref[(tile + 1) * tm + r] >> SRC_BITS
        return pltpu.make_async_copy(acc.at[tile % 2, pl.ds(r, 1)], y_hbm.at[pl.ds(dst, 1)], ssem)

    def wait_gather():
        pltpu.make_async_copy(h_hbm.at[pl.ds(0, tm)], xbuf, gsem).wait()

    def wait_scatter():
        pltpu.make_async_copy(acc.at[0], y_hbm.at[pl.ds(0, tm)], ssem).wait()

    def start_all(copy, tile):
        def body(r, c):
            copy(tile, r).start()
            return c
        lax.fori_loop(0, tm, body, 0)

    @pl.when((i == 0) & (f == 0))
    def _():
        acc[1] = jnp.zeros(acc.shape[1:], F32)
        fill = pltpu.make_async_copy(acc.at[1], y_hbm.at[pl.ds(dump0, tm)], ssem)
        fill.start()
        fill.wait()
        start_all(gather_copy, 0)

    @pl.when((f == 0) & (i <= nact))
    def _():
        wait_gather()

        @pl.when(i >= 1)
        def _():
            wait_scatter()

    def issue_chunk(c):
        for j in range(chunk):
            r = c * chunk + j
            gather_copy(i + 1, r).start()
            scatter_copy(i - 1, r).start()

    @pl.when((f == 0) & active)
    def _():
        xb[...] = xbuf[...].astype(BF16)
        acc[slot] = jnp.zeros(acc.shape[1:], F32)
        issue_chunk(0)

    @pl.when((f == 0) & (i == nact) & (i >= 1))
    def _():
        start_all(scatter_copy, i - 1)
        wait_scatter()

    def expert_step(m):
        issue_chunk(f + 1)
        x = xb[0:m, :]
        a = jnp.dot(x, wg_ref[0], preferred_element_type=F32)
        b = jnp.dot(x, wu_ref[0], preferred_element_type=F32)
        acc[slot, 0:m, :] += jnp.dot((_silu(a) * b).astype(BF16), wd_ref[0], preferred_element_type=F32)

    more_than_half = rows_ref[i] > tm // 2

    @pl.when(active & more_than_half)
    def _():
        expert_step(tm)

    @pl.when(active & jnp.logical_not(more_than_half))
    def _():
        expert_step(tm // 2)


def _moe(h, wg, wu, wd, tile_expert, n_active, tile_rows, route, n_out_rows, tm=MOE_TM):
    d = h.shape[1]
    ff = wg.shape[2]
    n_tiles = tile_expert.shape[0]
    nf = MOE_NF
    tf = ff // nf

    def fidx(i, f, nact):
        return jnp.where(i < nact[0], f, nf - 1)

    grid_spec = pltpu.PrefetchScalarGridSpec(
        num_scalar_prefetch=4,
        grid=(n_tiles, nf),
        in_specs=[pl.BlockSpec(memory_space=pl.ANY),
                  pl.BlockSpec((1, d, tf), lambda i, f, te, nact, rows, route: (te[i], 0, fidx(i, f, nact))),
                  pl.BlockSpec((1, d, tf), lambda i, f, te, nact, rows, route: (te[i], 0, fidx(i, f, nact))),
                  pl.BlockSpec((1, tf, d), lambda i, f, te, nact, rows, route: (te[i], fidx(i, f, nact), 0))],
        out_specs=pl.BlockSpec(memory_space=pl.ANY),
        scratch_shapes=[pltpu.VMEM((tm, d), F32), pltpu.VMEM((tm, d), BF16), pltpu.VMEM((2, tm, d), F32),
                        pltpu.SemaphoreType.DMA, pltpu.SemaphoreType.DMA],
    )
    return pl.pallas_call(
        _moe_kernel,
        grid_spec=grid_spec,
        out_shape=jax.ShapeDtypeStruct((n_out_rows, d), F32),
        compiler_params=_params(("arbitrary", "arbitrary")),
        name="moe",
    )(tile_expert, n_active, tile_rows, route, h, wg, wu, wd)


def _route_plan(r, n_tok, tm):
    n_assign = 2 * n_tok
    n_tiles = n_assign // tm + N_EXPERTS
    assert n_tiles * tm <= ITEM_RANGE
    e_flat = jnp.concatenate([r[:, 0], r[:, 1]]).astype(jnp.int32)
    experts = jnp.arange(N_EXPERTS, dtype=jnp.int32)
    counts = jnp.sum(e_flat[:, None] == experts[None, :], axis=0).astype(jnp.int32)
    tiles_per = (counts + tm - 1) // tm
    tend = jnp.cumsum(tiles_per)
    n_active = tend[-1]
    j = jnp.arange(n_tiles, dtype=jnp.int32)
    te = jnp.minimum(jnp.sum(j[:, None] >= tend[None, :], axis=1), N_EXPERTS - 1).astype(jnp.int32)
    last = jnp.sum(jnp.where(j == n_active - 1, te, 0))
    te = jnp.where(j < n_active, te, last)
    mine = te[:, None] == experts[None, :]
    first_tile = jnp.sum(jnp.where(mine, (tend - tiles_per)[None, :], 0), axis=1)
    group_rows = jnp.sum(jnp.where(mine, counts[None, :], 0), axis=1)
    tile_rows = jnp.where(j < n_active, jnp.clip(group_rows - (j - first_tile) * tm, 0, tm), 0).astype(jnp.int32)
    pad_id = jnp.arange(tm, dtype=jnp.int32)
    pad_on = pad_id[None, :] < (tiles_per * tm - counts)[:, None]
    pad_key = jnp.where(pad_on, 2 * experts[:, None] + 1, 2 * N_EXPERTS).reshape(-1)
    keys = jnp.concatenate([2 * e_flat, pad_key])
    row = jnp.arange(n_tiles * tm, dtype=jnp.int32)
    item = jnp.sort(keys * ITEM_RANGE + row) % ITEM_RANGE
    real = item < n_assign
    src = jnp.where(real, item % n_tok, 0)
    dst = jnp.where(real, item, n_assign + row % tm)
    body = src | (dst << SRC_BITS)
    edge = (jnp.arange(tm, dtype=jnp.int32) + n_assign) << SRC_BITS
    route = jnp.concatenate([edge, body, edge]).astype(jnp.int32)
    return te, n_active.reshape(1).astype(jnp.int32), tile_rows, route


def _combine_kernel(x_ref, y0_ref, y1_ref, r_ref, mod_ref, g_ref, o_ref):
    r = r_ref[...]
    y = r[:, 2:3] * y0_ref[...] + r[:, 3:4] * y1_ref[...]
    x = x_ref[...] + mod_ref[0, 5:6, :] * y
    ms = jnp.mean(x * x, axis=-1, keepdims=True)
    o_ref[...] = x * lax.rsqrt(ms + EPS) * g_ref[...]


def _combine(xs, y, r, mods, mod_index, g, tm=MM_TILE):
    t, d = xs.shape
    nb = t // tm
    return pl.pallas_call(
        _combine_kernel,
        grid=(nb,),
        in_specs=[pl.BlockSpec((tm, d), lambda i: (i, 0)),
                  pl.BlockSpec((tm, d), lambda i: (i, 0)),
                  pl.BlockSpec((tm, d), lambda i: (nb + i, 0)),
                  pl.BlockSpec((tm, LANES), lambda i: (i, 0)),
                  pl.BlockSpec((1, N_MOD, d), mod_index),
                  _resident((1, d))],
        out_specs=pl.BlockSpec((tm, d), lambda i: (i, 0)),
        out_shape=jax.ShapeDtypeStruct((t, d), F32),
        compiler_params=_params(("arbitrary",)),
        name="combine",
    )(xs, y, y, r, mods, g.reshape(1, d))


def _rope_tables(seq, extra_rows):
    rows = seq // GRID_W
    row = np.repeat(np.arange(rows), GRID_W).astype(np.float32)
    col = np.tile(np.arange(GRID_W), rows).astype(np.float32)
    inv = np.float32(ROPE_THETA) ** (-np.arange(ROPE_FREQS, dtype=np.float32) / np.float32(ROPE_FREQS))
    ar = row[:, None] * inv
    ac = col[:, None] * inv
    cos = np.ones((seq + extra_rows, LANES), np.float32)
    sin = np.zeros((seq + extra_rows, LANES), np.float32)
    cos[:seq, :QK_ROPE] = np.concatenate([np.cos(ar), np.cos(ar), np.cos(ac), np.cos(ac)], axis=1)
    sin[:seq, :QK_ROPE] = np.concatenate([-np.sin(ar), np.sin(ar), -np.sin(ac), np.sin(ac)], axis=1)
    return jnp.asarray(cos), jnp.asarray(sin)


def kernel(x, c, ctx, c_ctx, ada_w, ada_b, norm1_g, norm2_g, e_w_in, e_pool_w, e_pool_scale, e_conv_w, e_conv_b, e_conv_ln_g, e_conv_ln_b, e_w_out, e_ffn_w_gate, e_ffn_w_up, e_ffn_w_down, o_w_in, o_q_norm_g, o_w_qb, o_kv_norm_g, o_w_kvb, o_sgu_ln_g, o_sgu_ln_b, o_sgu_w, o_sgu_b, o_w_out, o_router_w, o_router_b, o_exp_w_gate, o_exp_w_up, o_exp_w_down, final_norm_g):
    n_batch, seq, d = x.shape
    ctx_len = ctx.shape[1]
    n_lat = n_batch * seq
    n_ctx = n_batch * ctx_len
    assert ctx_len % ROW_TILE == 0 and seq % MM_TILE == 0 and n_ctx % MM_TILE == 0
    assert ada_w.shape[0] == 2 and e_w_in.shape[0] == 1 and o_w_in.shape[0] == 1

    rows_per_layer = 8 * ((n_batch + 1 + 7) // 8)
    cond = jnp.zeros((rows_per_layer, d), F32).at[:n_batch].set(c).at[n_batch].set(c_ctx)
    mods = _ada(cond, ada_w, ada_b).reshape(2 * rows_per_layer, N_MOD, d)

    def mod_index(layer, tm):
        return _mod_spec(layer, tm, n_lat, seq, n_batch, rows_per_layer)

    x_lat = x.reshape(n_lat, d)
    x_ctx = ctx.reshape(n_ctx, d)

    p, (fg, fu, fd) = _inproj(x_lat, x_ctx, mods, mod_index(0, MM_TILE), norm1_g[0], e_w_in[0].astype(BF16),
                              [e_ffn_w_gate[0], e_ffn_w_up[0], e_ffn_w_down[0]])
    y = _poolconv(p, e_pool_w[0].astype(BF16), e_pool_scale[0], e_conv_w[0], e_conv_b[0],
                  e_conv_ln_g[0], e_conv_ln_b[0], n_lat, seq, ctx_len)
    x1, h2 = _outproj([y], e_w_out[0].astype(BF16), (x_lat, x_ctx), mods, mod_index(0, MM_TILE), norm2_g[0],
                      n_lat + n_ctx, BF16)
    x2 = _ffn(h2, fg, fu, fd, x1, mods, mod_index(0, MM_TILE))

    w_in = o_w_in[0]
    c_kr = Q_LORA + KV_LORA
    c_u = c_kr + QK_ROPE
    w_lora = w_in[:, :c_kr].astype(BF16)
    w_kr = jnp.pad(w_in[:, c_kr:c_u], ((0, 0), (0, LANES - QK_ROPE))).astype(BF16)
    w_sgu = w_in[:, c_u:].astype(BF16)
    wq = o_w_qb[0].reshape(Q_LORA, MLA_HEADS, QK_NOPE + QK_ROPE)
    wq = jnp.pad(wq, ((0, 0), (0, 0), (0, HEAD_PAD - QK_NOPE - QK_ROPE)))
    wq = wq.reshape(Q_LORA, MLA_HEADS * HEAD_PAD).astype(BF16)
    wkv = o_w_kvb[0].reshape(KV_LORA, MLA_HEADS, QK_NOPE + V_HEAD)
    wkv = jnp.concatenate([wkv[:, :, :QK_NOPE].reshape(KV_LORA, -1),
                           wkv[:, :, QK_NOPE:].reshape(KV_LORA, -1)], axis=1).astype(BF16)
    cos_t, sin_t = _rope_tables(seq, MM_TILE)
    d_sgu = o_sgu_ln_g.shape[1]
    sgu_bias = jnp.repeat(o_sgu_b[0].T, d_sgu // SGU_HEADS, axis=1)
    q, k, v, sg = _mla_proj(x2, mods, mod_index(1, MM_TILE), norm1_g[1], w_lora, w_kr, w_sgu,
                            o_q_norm_g[0], wq, o_kv_norm_g[0], wkv, cos_t, sin_t,
                            o_sgu_ln_g[0], o_sgu_ln_b[0], o_sgu_w[0].astype(BF16), sgu_bias,
                            n_lat, seq)
    experts = (o_exp_w_gate[0], o_exp_w_up[0], o_exp_w_down[0])
    attn, experts_bf16 = _attention(q, k, v, n_batch, seq, ctx_len,
                                    [w.reshape(-1, w.shape[-1]) for w in experts])
    wg, wu, wd = (wb.reshape(w.shape) for wb, w in zip(experts_bf16, experts))

    rw = jnp.pad(o_router_w[0], ((0, 0), (0, LANES - N_EXPERTS))).astype(BF16)
    rb = jnp.concatenate([o_router_b[0], jnp.full((LANES - N_EXPERTS,), NEG, F32)]).reshape(1, LANES)
    x3, h3, r = _outproj([attn, sg], o_w_out[0].astype(BF16), (x2,), mods, mod_index(1, MM_TILE),
                         norm2_g[1], n_lat, F32, router=(rw, rb))

    te, n_active, tile_rows, route = _route_plan(r, n_lat, MOE_TM)
    ys = _moe(h3, wg, wu, wd, te, n_active, tile_rows, route, 2 * n_lat + MOE_TM)
    out = _combine(x3, ys, r, mods, mod_index(1, MM_TILE), final_norm_g)
    return out.reshape(n_batch, seq, d)
```

```python
import functools

import jax
import jax.numpy as jnp
import numpy as np
from jax import lax
from jax.experimental import pallas as pl
from jax.experimental.pallas import tpu as pltpu

F32 = jnp.float32
BF16 = jnp.bfloat16

EPS = 1e-6
N_MOD = 6
GRID_W = 64
POOL_WINDOWS = (2, 4, 8, 16)
CONV_WIDTH = 31
MLA_HEADS = 8
Q_LORA = 512
KV_LORA = 512
QK_NOPE = 128
QK_ROPE = 64
V_HEAD = 128
ROPE_FREQS = QK_ROPE // 4
ROPE_THETA = 10000.0
SM_SCALE = (QK_NOPE + QK_ROPE) ** -0.5
LOG2E = 1.4426950408889634
SGU_HEADS = 8
CHUNK = 128
N_EXPERTS = 8

LANES = 128
SUBLANES = 8
BF16_ROWS = 16
HALO = 16
VMEM_LIMIT = 56 * 1024 * 1024
HEAD_PAD = 256
ROW_TILE = 256
MM_TILE = 512
FFN_TM = 1024
FFN_TF = 512
MOE_TM = 512
MOE_NF = 7
ATTN_HEADS_PER_STEP = 2
NEG = -1e30


def _params(sem):
    return pltpu.CompilerParams(dimension_semantics=sem, vmem_limit_bytes=VMEM_LIMIT)


def _resident(shape):
    nd = len(shape)
    return pl.BlockSpec(shape, lambda *_: (0,) * nd, pipeline_mode=pl.Buffered(1))


def _slab_specs(arrays, n_steps, step_of):
    specs = []
    for w in arrays:
        rows, cols = w.shape
        n = max(k for k in range(1, n_steps + 1) if rows % (BF16_ROWS * k) == 0)
        specs.append(pl.BlockSpec((rows // n, cols),
                                  lambda *idx, n=n: (jnp.minimum(step_of(*idx), n - 1), 0)))
    return specs


def _cast_slabs(in_refs, out_refs):
    for w_ref, wb_ref in zip(in_refs, out_refs):
        wb_ref[...] = w_ref[...].astype(BF16)


def _rmsnorm_mod(x, g, shift, scale):
    ms = jnp.mean(x * x, axis=-1, keepdims=True)
    return (x * lax.rsqrt(ms + EPS) * g) * (1.0 + scale) + shift


def _layernorm(x, g, b):
    mu = jnp.mean(x, axis=-1, keepdims=True)
    xc = x - mu
    var = jnp.mean(xc * xc, axis=-1, keepdims=True)
    return xc * lax.rsqrt(var + EPS) * g + b


def _silu(x):
    return x * jax.nn.sigmoid(x)


def _ada_kernel(c_ref, w_ref, b_ref, o_ref):
    s = _silu(c_ref[...]).astype(BF16)
    o_ref[0] = jnp.dot(s, w_ref[0].astype(BF16), preferred_element_type=F32) + b_ref[0]


def _ada(cond, ada_w, ada_b, tn=1024):
    depth, d, n = ada_w.shape
    rows = cond.shape[0]
    return pl.pallas_call(
        _ada_kernel,
        grid=(depth, n // tn),
        in_specs=[pl.BlockSpec((rows, d), lambda l, j: (0, 0)),
                  pl.BlockSpec((1, d, tn), lambda l, j: (l, 0, j)),
                  pl.BlockSpec((1, 1, tn), lambda l, j: (l, 0, j))],
        out_specs=pl.BlockSpec((1, rows, tn), lambda l, j: (l, 0, j)),
        out_shape=jax.ShapeDtypeStruct((depth, rows, n), F32),
        compiler_params=_params(("arbitrary", "arbitrary")),
        name="ada",
    )(cond, ada_w, ada_b.reshape(depth, 1, n))


def _mod_spec(layer, tm, n_lat_rows, seq, n_batch, rows_per_layer):
    def index(i, *_):
        r = jnp.where(i * tm < n_lat_rows, (i * tm) // seq, n_batch)
        return (layer * rows_per_layer + r, 0, 0)
    return index


def _pick_rows(i, n_lat_tiles, lat_ref, ctx_ref):
    return jnp.where(i < n_lat_tiles, lat_ref[...], ctx_ref[...])


def _two_source_specs(tm, d, n_lat_tiles):
    return [pl.BlockSpec((tm, d), lambda i: (jnp.minimum(i, n_lat_tiles - 1), 0)),
            pl.BlockSpec((tm, d), lambda i: (jnp.maximum(i - n_lat_tiles, 0), 0))]


def _inproj_kernel(*refs, n_lat_tiles, n_cast):
    xl_ref, xc_ref, mod_ref, g_ref, w_ref = refs[:5]
    o_ref = refs[5 + n_cast]
    x = _pick_rows(pl.program_id(0), n_lat_tiles, xl_ref, xc_ref)
    h = _rmsnorm_mod(x, g_ref[...], mod_ref[0, 0:1, :], mod_ref[0, 1:2, :])
    o_ref[...] = jnp.dot(h.astype(BF16), w_ref[...], preferred_element_type=F32).astype(o_ref.dtype)
    _cast_slabs(refs[5:5 + n_cast], refs[6 + n_cast:])


def _inproj(x_lat, x_ctx, mods, mod_index, g, w, casts, tm=MM_TILE):
    d = x_lat.shape[1]
    t = x_lat.shape[0] + x_ctx.shape[0]
    n = w.shape[1]
    n_lat_tiles = x_lat.shape[0] // tm
    cast_specs = _slab_specs(casts, t // tm, lambda i: i)
    outs = pl.pallas_call(
        functools.partial(_inproj_kernel, n_lat_tiles=n_lat_tiles, n_cast=len(casts)),
        grid=(t // tm,),
        in_specs=_two_source_specs(tm, d, n_lat_tiles) + [
            pl.BlockSpec((1, N_MOD, d), mod_index), _resident((1, d)), _resident((d, n))] + cast_specs,
        out_specs=[pl.BlockSpec((tm, n), lambda i: (i, 0))] + cast_specs,
        out_shape=[jax.ShapeDtypeStruct((t, n), BF16)] + [jax.ShapeDtypeStruct(c.shape, BF16) for c in casts],
        compiler_params=_params(("arbitrary",)),
        name="inproj0",
    )(x_lat, x_ctx, mods, g.reshape(1, d), w, *casts)
    return outs[0], outs[1:]


def _poolconv_kernel(up_ref, um_ref, un_ref, ap_ref, am_ref, an_ref, gp_ref, gm_ref, gn_ref,
                     pw_ref, ps_ref, cw_ref, cb_ref, lg_ref, lb_ref, y_ref,
                     ubuf, zbuf, zsh, cbuf, *, n_lat_tiles, seq, ctx_len):
    i = pl.program_id(0)
    tm = um_ref.shape[0]
    dp = um_ref.shape[1]
    is_lat = i < n_lat_tiles
    pos0 = jnp.where(is_lat, (i * tm) % seq, ((i - n_lat_tiles) * tm) % ctx_len)
    length = jnp.where(is_lat, seq, ctx_len)
    keep_p = jnp.where(pos0 == 0, 0.0, 1.0).astype(F32)
    keep_n = jnp.where(pos0 + tm == length, 0.0, 1.0).astype(F32)

    def glu(a_ref, g_ref):
        return a_ref[...].astype(F32) * jax.nn.sigmoid(g_ref[...].astype(F32))

    ubuf[0:HALO, :] = up_ref[...].astype(F32) * keep_p
    ubuf[HALO:HALO + tm, :] = um_ref[...].astype(F32)
    ubuf[HALO + tm:, :] = un_ref[...].astype(F32) * keep_n
    zbuf[0:HALO, :] = glu(ap_ref, gp_ref) * keep_p
    zbuf[HALO:HALO + tm, :] = glu(am_ref, gm_ref)
    zbuf[HALO + tm:, :] = glu(an_ref, gn_ref) * keep_n

    pos = pos0 + lax.broadcasted_iota(jnp.int32, (tm, 1), 0)
    gdim = dp // len(POOL_WINDOWS)
    for g, w in enumerate(POOL_WINDOWS):
        cols = slice(g * gdim, (g + 1) * gdim)
        s = ubuf[HALO - w // 2:HALO - w // 2 + tm, cols]
        for o in range(-w // 2 + 1, w // 2):
            s = s + ubuf[HALO + o:HALO + o + tm, cols]
        cnt = jnp.minimum(pos - w // 2 + w, length) - jnp.maximum(pos - w // 2, 0)
        pooled = s / cnt.astype(F32) - ubuf[HALO:HALO + tm, cols]
        mixed = jnp.dot(pooled.astype(BF16), pw_ref[g], preferred_element_type=F32)
        y_ref[:, cols] = (mixed * ps_ref[:, cols]).astype(y_ref.dtype)

    n_sh = zsh.shape[1]
    for j in range(1, SUBLANES):
        zsh[j - 1] = zbuf[j:j + n_sh, :]
    rb = 64
    first = HALO - CONV_WIDTH // 2
    for c in range(0, zbuf.shape[1], LANES):
        cols = slice(c, c + LANES)
        taps = [cw_ref[k:k + 1, cols] for k in range(CONV_WIDTH)]
        for r in range(0, tm, rb):
            acc = None
            for k in range(CONV_WIDTH):
                q, j = divmod(first + k, SUBLANES)
                rows = slice(q * SUBLANES + r, q * SUBLANES + r + rb)
                z = zbuf[rows, cols] if j == 0 else zsh[j - 1, rows, cols]
                acc = taps[k] * z if acc is None else acc + taps[k] * z
            cbuf[r:r + rb, cols] = acc
    conv = _layernorm(cbuf[...] + cb_ref[...], lg_ref[...], lb_ref[...])
    y_ref[:, dp:] = _silu(conv).astype(y_ref.dtype)


def _poolconv(p, pool_w, pool_scale, conv_w, conv_b, ln_g, ln_b, n_lat_rows, seq, ctx_len,
              tm=ROW_TILE):
    t = p.shape[0]
    dp = pool_scale.shape[0]
    dc = conv_b.shape[0]
    hb = tm // HALO
    last_hb = t // HALO - 1

    def main(c):
        return pl.BlockSpec((tm, dp), lambda i: (i, c))

    def prev(c):
        return pl.BlockSpec((HALO, dp), lambda i: (jnp.maximum(i * hb - 1, 0), c))

    def nxt(c):
        return pl.BlockSpec((HALO, dp), lambda i: (jnp.minimum((i + 1) * hb, last_hb), c))

    kern = functools.partial(_poolconv_kernel, n_lat_tiles=n_lat_rows // tm, seq=seq, ctx_len=ctx_len)
    return pl.pallas_call(
        kern,
        grid=(t // tm,),
        in_specs=[prev(0), main(0), nxt(0), prev(1), main(1), nxt(1), prev(2), main(2), nxt(2),
                  _resident(pool_w.shape), _resident((1, dp)), _resident(conv_w.shape),
                  _resident((1, dc)), _resident((1, dc)), _resident((1, dc))],
        out_specs=pl.BlockSpec((tm, dp + dc), lambda i: (i, 0)),
        out_shape=jax.ShapeDtypeStruct((t, dp + dc), BF16),
        scratch_shapes=[pltpu.VMEM((tm + 2 * HALO, dp), F32),
                        pltpu.VMEM((tm + 2 * HALO, dc), F32),
                        pltpu.VMEM((SUBLANES - 1, tm + 2 * HALO - SUBLANES, dc), F32),
                        pltpu.VMEM((tm, dc), F32)],
        compiler_params=_params(("arbitrary",)),
        name="poolconv",
    )(p, p, p, p, p, p, p, p, p, pool_w, pool_scale.reshape(1, dp), conv_w,
      conv_b.reshape(1, dc), ln_g.reshape(1, dc), ln_b.reshape(1, dc))


def _outproj_kernel(*refs, n_in, n_res, n_lat_tiles, router):
    y_refs = refs[:n_in]
    w_ref = refs[n_in]
    x_refs = refs[n_in + 1:n_in + 1 + n_res]
    mod_ref, g_ref = refs[n_in + 1 + n_res:n_in + 3 + n_res]
    rest = refs[n_in + 3 + n_res:]
    if router:
        rw_ref, rb_ref, xo_ref, ho_ref, r_ref = rest
    else:
        xo_ref, ho_ref = rest
    o = None
    k0 = 0
    for y_ref in y_refs:
        kk = y_ref.shape[1]
        part = jnp.dot(y_ref[...], w_ref[k0:k0 + kk, :], preferred_element_type=F32)
        o = part if o is None else o + part
        k0 += kk
    if n_res == 2:
        x = _pick_rows(pl.program_id(0), n_lat_tiles, x_refs[0], x_refs[1])
    else:
        x = x_refs[0][...]
    x1 = x + mod_ref[0, 2:3, :] * o
    xo_ref[...] = x1
    h = _rmsnorm_mod(x1, g_ref[...], mod_ref[0, 3:4, :], mod_ref[0, 4:5, :])
    ho_ref[...] = h.astype(ho_ref.dtype)
    if router:
        logits = jnp.dot(h.astype(BF16), rw_ref[...], preferred_element_type=F32) + rb_ref[...]
        lane = lax.broadcasted_iota(jnp.int32, logits.shape, 1)
        m1 = jnp.max(logits, axis=-1, keepdims=True)
        i1 = jnp.min(jnp.where(logits == m1, lane, LANES), axis=-1, keepdims=True)
        rest_l = jnp.where(lane == i1, NEG * 2, logits)
        m2 = jnp.max(rest_l, axis=-1, keepdims=True)
        i2 = jnp.min(jnp.where(rest_l == m2, lane, LANES), axis=-1, keepdims=True)
        e2 = jnp.exp(m2 - m1)
        p1 = 1.0 / (1.0 + e2)
        p2 = e2 / (1.0 + e2)
        r = jnp.where(lane == 0, i1.astype(F32), 0.0)
        r = jnp.where(lane == 1, i2.astype(F32), r)
        r = jnp.where(lane == 2, p1, r)
        r = jnp.where(lane == 3, p2, r)
        r_ref[...] = r


def _outproj(ys, w, res, mods, mod_index, g, n_rows, h_dtype, router=None, tm=MM_TILE):
    d = w.shape[1]
    in_specs = [pl.BlockSpec((tm, y.shape[1]), lambda i: (i, 0)) for y in ys]
    in_specs.append(_resident(w.shape))
    n_lat_tiles = res[0].shape[0] // tm
    if len(res) == 2:
        in_specs += _two_source_specs(tm, d, n_lat_tiles)
    else:
        in_specs.append(pl.BlockSpec((tm, d), lambda i: (i, 0)))
    in_specs += [pl.BlockSpec((1, N_MOD, d), mod_index), _resident((1, d))]
    args = list(ys) + [w] + list(res) + [mods, g.reshape(1, d)]
    out_specs = [pl.BlockSpec((tm, d), lambda i: (i, 0)), pl.BlockSpec((tm, d), lambda i: (i, 0))]
    out_shape = [jax.ShapeDtypeStruct((n_rows, d), F32), jax.ShapeDtypeStruct((n_rows, d), h_dtype)]
    if router is not None:
        rw, rb = router
        in_specs += [_resident(rw.shape), _resident(rb.shape)]
        args += [rw, rb]
        out_specs.append(pl.BlockSpec((tm, LANES), lambda i: (i, 0)))
        out_shape.append(jax.ShapeDtypeStruct((n_rows, LANES), F32))
    kern = functools.partial(_outproj_kernel, n_in=len(ys), n_res=len(res), n_lat_tiles=n_lat_tiles,
                             router=router is not None)
    return pl.pallas_call(
        kern,
        grid=(n_rows // tm,),
        in_specs=in_specs,
        out_specs=out_specs,
        out_shape=out_shape,
        compiler_params=_params(("arbitrary",)),
        name="outproj_router" if router is not None else "outproj",
    )(*args)


def _ffn_kernel(h_ref, wg_ref, wu_ref, wd_ref, x_hbm, mod_ref, o_ref, xres, sem):
    i = pl.program_id(0)
    f = pl.program_id(1)
    tm = o_ref.shape[0]
    residual = pltpu.make_async_copy(x_hbm.at[pl.ds(pl.multiple_of(i * tm, tm), tm)], xres, sem)

    @pl.when(f == 0)
    def _():
        residual.start()
        o_ref[...] = jnp.zeros(o_ref.shape, F32)

    h = h_ref[...]
    a = jnp.dot(h, wg_ref[...], preferred_element_type=F32)
    b = jnp.dot(h, wu_ref[...], preferred_element_type=F32)
    o_ref[...] += jnp.dot((_silu(a) * b).astype(BF16), wd_ref[...], preferred_element_type=F32)

    @pl.when(f == pl.num_programs(1) - 1)
    def _():
        residual.wait()
        o_ref[...] = xres[...] + mod_ref[0, 5:6, :] * o_ref[...]


def _ffn(h, wg, wu, wd, xs, mods, mod_index, tm=FFN_TM, tf=FFN_TF):
    t, d = xs.shape
    ff = wg.shape[1]
    return pl.pallas_call(
        _ffn_kernel,
        grid=(t // tm, ff // tf),
        in_specs=[pl.BlockSpec((tm, d), lambda i, f: (i, 0)),
                  pl.BlockSpec((d, tf), lambda i, f: (0, f)),
                  pl.BlockSpec((d, tf), lambda i, f: (0, f)),
                  pl.BlockSpec((tf, d), lambda i, f: (f, 0)),
                  pl.BlockSpec(memory_space=pl.ANY),
                  pl.BlockSpec((1, N_MOD, d), mod_index)],
        out_specs=pl.BlockSpec((tm, d), lambda i, f: (i, 0)),
        out_shape=jax.ShapeDtypeStruct((t, d), F32),
        scratch_shapes=[pltpu.VMEM((tm, d), F32), pltpu.SemaphoreType.DMA],
        compiler_params=_params(("arbitrary", "arbitrary")),
        name="ffn",
    )(h, wg, wu, wd, xs, mods)


def _rope(x, c, s):
    lane = lax.broadcasted_iota(jnp.int32, x.shape, 1)
    partner = jnp.where(lane % 32 < 16, pltpu.roll(x, LANES - 16, 1), pltpu.roll(x, 16, 1))
    return x * c + partner * s


def _mla_kernel(x_ref, mod_ref, g_ref, wlora_ref, wkr_ref, wsgu_ref, qg_ref, wq_ref, kvg_ref, wkv_ref,
                cos_ref, sin_ref, slg_ref, slb_ref, sw_ref, sb_ref,
                q_ref, k_ref, v_ref, sg_ref, *, n_lat_tiles):
    i = pl.program_id(0)
    tm = x_ref.shape[0]
    h = _rmsnorm_mod(x_ref[...], g_ref[...], mod_ref[0, 0:1, :], mod_ref[0, 1:2, :]).astype(BF16)
    p_lora = jnp.dot(h, wlora_ref[...], preferred_element_type=F32)
    p_sgu = jnp.dot(h, wsgu_ref[...], preferred_element_type=F32)
    d_sgu = slg_ref.shape[1]
    cos = cos_ref[...]
    sin = sin_ref[...]

    def rms(x, g):
        return x * lax.rsqrt(jnp.mean(x * x, axis=-1, keepdims=True) + EPS) * g

    ckv = rms(p_lora[:, Q_LORA:], kvg_ref[...]).astype(BF16)
    kv = jnp.dot(ckv, wkv_ref[...], preferred_element_type=F32)
    kr = _rope(jnp.dot(h, wkr_ref[...], preferred_element_type=F32), cos, sin).astype(BF16)
    nk = MLA_HEADS * QK_NOPE
    for hd in range(MLA_HEADS):
        k_ref[:, hd * HEAD_PAD:hd * HEAD_PAD + QK_NOPE] = kv[:, hd * QK_NOPE:(hd + 1) * QK_NOPE].astype(BF16)
        k_ref[:, hd * HEAD_PAD + QK_NOPE:(hd + 1) * HEAD_PAD] = kr
        v_ref[:, hd * HEAD_PAD:hd * HEAD_PAD + V_HEAD] = kv[:, nk + hd * V_HEAD:nk + (hd + 1) * V_HEAD].astype(BF16)
        v_ref[:, hd * HEAD_PAD + V_HEAD:(hd + 1) * HEAD_PAD] = jnp.ones((tm, HEAD_PAD - V_HEAD), BF16)

    @pl.when(i < n_lat_tiles)
    def _():
        cq = rms(p_lora[:, :Q_LORA], qg_ref[...]).astype(BF16)
        q = jnp.dot(cq, wq_ref[...], preferred_element_type=F32) * (SM_SCALE * LOG2E)
        for hd in range(MLA_HEADS):
            b0 = hd * HEAD_PAD
            q_ref[:, b0:b0 + QK_NOPE] = q[:, b0:b0 + QK_NOPE].astype(BF16)
            q_ref[:, b0 + QK_NOPE:b0 + HEAD_PAD] = _rope(q[:, b0 + QK_NOPE:b0 + HEAD_PAD], cos, sin).astype(BF16)

        zu = jax.nn.gelu(p_sgu[:, :d_sgu])
        zv = _layernorm(jax.nn.gelu(p_sgu[:, d_sgu:]), slg_ref[...], slb_ref[...]).astype(BF16)
        hdim = d_sgu // SGU_HEADS
        for ck in range(tm // CHUNK):
            rows = slice(ck * CHUNK, (ck + 1) * CHUNK)
            for hd in range(SGU_HEADS):
                cols = slice(hd * hdim, (hd + 1) * hdim)
                mixed = jnp.dot(sw_ref[hd], zv[rows, cols], preferred_element_type=F32) + sb_ref[:, cols]
                sg_ref[rows, cols] = (zu[rows, cols] * mixed).astype(BF16)


def _mla_proj(xs, mods, mod_index, g, w_lora, w_kr, w_sgu, qg, wq, kvg, wkv, cos_t, sin_t, slg, slb, sw, sb,
              n_lat_rows, seq, tm=MM_TILE):
    t, d = xs.shape
    n_lat_tiles = n_lat_rows // tm
    seq_tiles = seq // tm
    d_sgu = slg.shape[0]

    def lat(i):
        return (jnp.minimum(i, n_lat_tiles - 1), 0)

    def table(i):
        return (jnp.where(i < n_lat_tiles, i % seq_tiles, seq_tiles), 0)

    kern = functools.partial(_mla_kernel, n_lat_tiles=n_lat_tiles)
    return pl.pallas_call(
        kern,
        grid=(t // tm,),
        in_specs=[pl.BlockSpec((tm, d), lambda i: (i, 0)),
                  pl.BlockSpec((1, N_MOD, d), mod_index),
                  _resident((1, d)), _resident(w_lora.shape), _resident(w_kr.shape), _resident(w_sgu.shape),
                  _resident((1, Q_LORA)), _resident(wq.shape),
                  _resident((1, KV_LORA)), _resident(wkv.shape),
                  pl.BlockSpec((tm, LANES), table), pl.BlockSpec((tm, LANES), table),
                  _resident((1, d_sgu)), _resident((1, d_sgu)), _resident(sw.shape), _resident(sb.shape)],
        out_specs=[pl.BlockSpec((tm, MLA_HEADS * HEAD_PAD), lat),
                   pl.BlockSpec((tm, MLA_HEADS * HEAD_PAD), lambda i: (i, 0)),
                   pl.BlockSpec((tm, MLA_HEADS * HEAD_PAD), lambda i: (i, 0)),
                   pl.BlockSpec((tm, d_sgu), lat)],
        out_shape=[jax.ShapeDtypeStruct((n_lat_rows, MLA_HEADS * HEAD_PAD), BF16),
                   jax.ShapeDtypeStruct((t, MLA_HEADS * HEAD_PAD), BF16),
                   jax.ShapeDtypeStruct((t, MLA_HEADS * HEAD_PAD), BF16),
                   jax.ShapeDtypeStruct((n_lat_rows, d_sgu), BF16)],
        compiler_params=_params(("arbitrary",)),
        name="mla_proj",
    )(xs, mods, g.reshape(1, d), w_lora, w_kr, w_sgu, qg.reshape(1, -1), wq, kvg.reshape(1, -1), wkv,
      cos_t, sin_t, slg.reshape(1, -1), slb.reshape(1, -1), sw, sb)


def _attn_kernel(*refs, n_cast):
    q_ref, kc_ref, kl_ref, vc_ref, vl_ref = refs[:5]
    cast_in = refs[5:5 + n_cast]
    o_ref = refs[5 + n_cast]
    cast_out = refs[6 + n_cast:]
    dn = (((1,), (1,)), ((), ()))

    def scores(hd):
        qk = slice(hd * HEAD_PAD, (hd + 1) * HEAD_PAD)
        q = q_ref[:, qk]
        return (lax.dot_general(q, kc_ref[:, qk], dn, preferred_element_type=F32),
                lax.dot_general(q, kl_ref[:, qk], dn, preferred_element_type=F32))

    def finish(hd, sc, sl):
        vv = slice(hd * HEAD_PAD, (hd + 1) * HEAD_PAD)
        m = jnp.maximum(jnp.max(sc, axis=-1, keepdims=True), jnp.max(sl, axis=-1, keepdims=True))
        pc = jnp.exp2((sc - m).astype(BF16))
        pq = jnp.exp2((sl - m).astype(BF16))
        o = (jnp.dot(pc, vc_ref[:, vv], preferred_element_type=F32)
             + jnp.dot(pq, vl_ref[:, vv], preferred_element_type=F32))
        out = slice(hd * V_HEAD, (hd + 1) * V_HEAD)
        o_ref[:, out] = (o[:, :V_HEAD] / o[:, V_HEAD:V_HEAD + 1]).astype(o_ref.dtype)

    s_next = scores(0)
    for hd in range(ATTN_HEADS_PER_STEP):
        s_cur = s_next
        if hd + 1 < ATTN_HEADS_PER_STEP:
            s_next = scores(hd + 1)
        finish(hd, *s_cur)
    _cast_slabs(cast_in, cast_out)


def _attention(q, k, v, n_batch, seq, ctx_len, casts, tq=ROW_TILE):
    n_lat_rows = n_batch * seq
    qb = seq // tq
    ctx_blk0 = n_lat_rows // ctx_len
    hs = ATTN_HEADS_PER_STEP
    hp = MLA_HEADS // hs
    n_steps = n_batch * hp * qb

    cast_specs = _slab_specs(casts, n_steps, lambda b, h, j: (b * hp + h) * qb + j)
    outs = pl.pallas_call(
        functools.partial(_attn_kernel, n_cast=len(casts)),
        grid=(n_batch, hp, qb),
        in_specs=[pl.BlockSpec((tq, hs * HEAD_PAD), lambda b, h, j: (b * qb + j, h)),
                  pl.BlockSpec((ctx_len, hs * HEAD_PAD), lambda b, h, j: (ctx_blk0 + b, h)),
                  pl.BlockSpec((seq, hs * HEAD_PAD), lambda b, h, j: (b, h)),
                  pl.BlockSpec((ctx_len, hs * HEAD_PAD), lambda b, h, j: (ctx_blk0 + b, h)),
                  pl.BlockSpec((seq, hs * HEAD_PAD), lambda b, h, j: (b, h))] + cast_specs,
        out_specs=[pl.BlockSpec((tq, hs * V_HEAD), lambda b, h, j: (b * qb + j, h))] + cast_specs,
        out_shape=[jax.ShapeDtypeStruct((n_lat_rows, MLA_HEADS * V_HEAD), BF16)]
        + [jax.ShapeDtypeStruct(w.shape, BF16) for w in casts],
        compiler_params=_params(("arbitrary", "arbitrary", "arbitrary")),
        name="attention",
    )(q, k, k, v, v, *casts)
    return outs[0], outs[1:]


SRC_BITS = 15
ITEM_RANGE = 1 << 16


def _moe_kernel(te_ref, nact_ref, rows_ref, route_ref, h_hbm, wg_ref, wu_ref, wd_ref, y_hbm,
                xbuf, xb, acc, gsem, ssem):
    i = pl.program_id(0)
    f = pl.program_id(1)
    tm = xbuf.shape[0]
    chunk = tm // (MOE_NF + 1)
    nact = nact_ref[0]
    active = i < nact
    slot = i % 2
    dump0 = y_hbm.shape[0] - tm

    def gather_copy(tile, r):
        src = route_ref[(tile + 1) * tm + r] & ((1 << SRC_BITS) - 1)
        return pltpu.make_async_copy(h_hbm.at[pl.ds(src, 1)], xbuf.at[pl.ds(r, 1)], gsem)

    def scatter_copy(tile, r):
        dst = route_ref[(tile + 1) * tm + r] >> SRC_BITS
        return pltpu.make_async_copy(acc.at[tile % 2, pl.ds(r, 1)], y_hbm.at[pl.ds(dst, 1)], ssem)

    def wait_gather():
        pltpu.make_async_copy(h_hbm.at[pl.ds(0, tm)], xbuf, gsem).wait()

    def wait_scatter():
        pltpu.make_async_copy(acc.at[0], y_hbm.at[pl.ds(0, tm)], ssem).wait()

    def start_all(copy, tile):
        def body(r, c):
            copy(tile, r).start()
            return c
        lax.fori_loop(0, tm, body, 0)

    @pl.when((i == 0) & (f == 0))
    def _():
        acc[1] = jnp.zeros(acc.shape[1:], F32)
        fill = pltpu.make_async_copy(acc.at[1], y_hbm.at[pl.ds(dump0, tm)], ssem)
        fill.start()
        fill.wait()
        start_all(gather_copy, 0)

    @pl.when((f == 0) & (i <= nact))
    def _():
        wait_gather()

        @pl.when(i >= 1)
        def _():
            wait_scatter()

    def issue_chunk(c):
        for j in range(chunk):
            r = c * chunk + j
            gather_copy(i + 1, r).start()
            scatter_copy(i - 1, r).start()

    @pl.when((f == 0) & active)
    def _():
        xb[...] = xbuf[...].astype(BF16)
        acc[slot] = jnp.zeros(acc.shape[1:], F32)
        issue_chunk(0)

    @pl.when((f == 0) & (i == nact) & (i >= 1))
    def _():
        start_all(scatter_copy, i - 1)
        wait_scatter()

    def expert_step(m):
        issue_chunk(f + 1)
        x = xb[0:m, :]
        a = jnp.dot(x, wg_ref[0], preferred_element_type=F32)
        b = jnp.dot(x, wu_ref[0], preferred_element_type=F32)
        acc[slot, 0:m, :] += jnp.dot((_silu(a) * b).astype(BF16), wd_ref[0], preferred_element_type=F32)

    more_than_half = rows_ref[i] > tm // 2

    @pl.when(active & more_than_half)
    def _():
        expert_step(tm)

    @pl.when(active & jnp.logical_not(more_than_half))
    def _():
        expert_step(tm // 2)


def _moe(h, wg, wu, wd, tile_expert, n_active, tile_rows, route, n_out_rows, tm=MOE_TM):
    d = h.shape[1]
    ff = wg.shape[2]
    n_tiles = tile_expert.shape[0]
    nf = MOE_NF
    tf = ff // nf

    def fidx(i, f, nact):
        return jnp.where(i < nact[0], f, nf - 1)

    grid_spec = pltpu.PrefetchScalarGridSpec(
        num_scalar_prefetch=4,
        grid=(n_tiles, nf),
        in_specs=[pl.BlockSpec(memory_space=pl.ANY),
                  pl.BlockSpec((1, d, tf), lambda i, f, te, nact, rows, route: (te[i], 0, fidx(i, f, nact))),
                  pl.BlockSpec((1, d, tf), lambda i, f, te, nact, rows, route: (te[i], 0, fidx(i, f, nact))),
                  pl.BlockSpec((1, tf, d), lambda i, f, te, nact, rows, route: (te[i], fidx(i, f, nact), 0))],
        out_specs=pl.BlockSpec(memory_space=pl.ANY),
        scratch_shapes=[pltpu.VMEM((tm, d), F32), pltpu.VMEM((tm, d), BF16), pltpu.VMEM((2, tm, d), F32),
                        pltpu.SemaphoreType.DMA, pltpu.SemaphoreType.DMA],
    )
    return pl.pallas_call(
        _moe_kernel,
        grid_spec=grid_spec,
        out_shape=jax.ShapeDtypeStruct((n_out_rows, d), F32),
        compiler_params=_params(("arbitrary", "arbitrary")),
        name="moe",
    )(tile_expert, n_active, tile_rows, route, h, wg, wu, wd)


def _route_plan(r, n_tok, tm):
    n_assign = 2 * n_tok
    n_tiles = n_assign // tm + N_EXPERTS
    assert n_tiles * tm <= ITEM_RANGE
    e_flat = jnp.concatenate([r[:, 0], r[:, 1]]).astype(jnp.int32)
    experts = jnp.arange(N_EXPERTS, dtype=jnp.int32)
    counts = jnp.sum(e_flat[:, None] == experts[None, :], axis=0).astype(jnp.int32)
    tiles_per = (counts + tm - 1) // tm
    tend = jnp.cumsum(tiles_per)
    n_active = tend[-1]
    j = jnp.arange(n_tiles, dtype=jnp.int32)
    te = jnp.minimum(jnp.sum(j[:, None] >= tend[None, :], axis=1), N_EXPERTS - 1).astype(jnp.int32)
    last = jnp.sum(jnp.where(j == n_active - 1, te, 0))
    te = jnp.where(j < n_active, te, last)
    mine = te[:, None] == experts[None, :]
    first_tile = jnp.sum(jnp.where(mine, (tend - tiles_per)[None, :], 0), axis=1)
    group_rows = jnp.sum(jnp.where(mine, counts[None, :], 0), axis=1)
    tile_rows = jnp.where(j < n_active, jnp.clip(group_rows - (j - first_tile) * tm, 0, tm), 0).astype(jnp.int32)
    pad_id = jnp.arange(tm, dtype=jnp.int32)
    pad_on = pad_id[None, :] < (tiles_per * tm - counts)[:, None]
    pad_key = jnp.where(pad_on, 2 * experts[:, None] + 1, 2 * N_EXPERTS).reshape(-1)
    keys = jnp.concatenate([2 * e_flat, pad_key])
    row = jnp.arange(n_tiles * tm, dtype=jnp.int32)
    item = jnp.sort(keys * ITEM_RANGE + row) % ITEM_RANGE
    real = item < n_assign
    src = jnp.where(real, item % n_tok, 0)
    dst = jnp.where(real, item, n_assign + row % tm)
    body = src | (dst << SRC_BITS)
    edge = (jnp.arange(tm, dtype=jnp.int32) + n_assign) << SRC_BITS
    route = jnp.concatenate([edge, body, edge]).astype(jnp.int32)
    return te, n_active.reshape(1).astype(jnp.int32), tile_rows, route


def _combine_kernel(x_ref, y0_ref, y1_ref, r_ref, mod_ref, g_ref, o_ref):
    r = r_ref[...]
    y = r[:, 2:3] * y0_ref[...] + r[:, 3:4] * y1_ref[...]
    x = x_ref[...] + mod_ref[0, 5:6, :] * y
    ms = jnp.mean(x * x, axis=-1, keepdims=True)
    o_ref[...] = x * lax.rsqrt(ms + EPS) * g_ref[...]


def _combine(xs, y, r, mods, mod_index, g, tm=MM_TILE):
    t, d = xs.shape
    nb = t // tm
    return pl.pallas_call(
        _combine_kernel,
        grid=(nb,),
        in_specs=[pl.BlockSpec((tm, d), lambda i: (i, 0)),
                  pl.BlockSpec((tm, d), lambda i: (i, 0)),
                  pl.BlockSpec((tm, d), lambda i: (nb + i, 0)),
                  pl.BlockSpec((tm, LANES), lambda i: (i, 0)),
                  pl.BlockSpec((1, N_MOD, d), mod_index),
                  _resident((1, d))],
        out_specs=pl.BlockSpec((tm, d), lambda i: (i, 0)),
        out_shape=jax.ShapeDtypeStruct((t, d), F32),
        compiler_params=_params(("arbitrary",)),
        name="combine",
    )(xs, y, y, r, mods, g.reshape(1, d))


def _rope_tables(seq, extra_rows):
    rows = seq // GRID_W
    row = np.repeat(np.arange(rows), GRID_W).astype(np.float32)
    col = np.tile(np.arange(GRID_W), rows).astype(np.float32)
    inv = np.float32(ROPE_THETA) ** (-np.arange(ROPE_FREQS, dtype=np.float32) / np.float32(ROPE_FREQS))
    ar = row[:, None] * inv
    ac = col[:, None] * inv
    cos = np.ones((seq + extra_rows, LANES), np.float32)
    sin = np.zeros((seq + extra_rows, LANES), np.float32)
    cos[:seq, :QK_ROPE] = np.concatenate([np.cos(ar), np.cos(ar), np.cos(ac), np.cos(ac)], axis=1)
    sin[:seq, :QK_ROPE] = np.concatenate([-np.sin(ar), np.sin(ar), -np.sin(ac), np.sin(ac)], axis=1)
    return jnp.asarray(cos), jnp.asarray(sin)


def kernel(x, c, ctx, c_ctx, ada_w, ada_b, norm1_g, norm2_g, e_w_in, e_pool_w, e_pool_scale, e_conv_w, e_conv_b, e_conv_ln_g, e_conv_ln_b, e_w_out, e_ffn_w_gate, e_ffn_w_up, e_ffn_w_down, o_w_in, o_q_norm_g, o_w_qb, o_kv_norm_g, o_w_kvb, o_sgu_ln_g, o_sgu_ln_b, o_sgu_w, o_sgu_b, o_w_out, o_router_w, o_router_b, o_exp_w_gate, o_exp_w_up, o_exp_w_down, final_norm_g):
    n_batch, seq, d = x.shape
    ctx_len = ctx.shape[1]
    n_lat = n_batch * seq
    n_ctx = n_batch * ctx_len
    assert ctx_len % ROW_TILE == 0 and seq % FFN_TM == 0 and n_ctx % FFN_TM == 0
    assert ada_w.shape[0] == 2 and e_w_in.shape[0] == 1 and o_w_in.shape[0] == 1

    rows_per_layer = 8 * ((n_batch + 1 + 7) // 8)
    cond = jnp.zeros((rows_per_layer, d), F32).at[:n_batch].set(c).at[n_batch].set(c_ctx)
    mods = _ada(cond, ada_w, ada_b).reshape(2 * rows_per_layer, N_MOD, d)

    def mod_index(layer, tm):
        return _mod_spec(layer, tm, n_lat, seq, n_batch, rows_per_layer)

    x_lat = x.reshape(n_lat, d)
    x_ctx = ctx.reshape(n_ctx, d)

    p, (fg, fu, fd) = _inproj(x_lat, x_ctx, mods, mod_index(0, MM_TILE), norm1_g[0], e_w_in[0].astype(BF16),
                              [e_ffn_w_gate[0], e_ffn_w_up[0], e_ffn_w_down[0]])
    y = _poolconv(p, e_pool_w[0].astype(BF16), e_pool_scale[0], e_conv_w[0], e_conv_b[0],
                  e_conv_ln_g[0], e_conv_ln_b[0], n_lat, seq, ctx_len)
    x1, h2 = _outproj([y], e_w_out[0].astype(BF16), (x_lat, x_ctx), mods, mod_index(0, MM_TILE), norm2_g[0],
                      n_lat + n_ctx, BF16)
    x2 = _ffn(h2, fg, fu, fd, x1, mods, mod_index(0, FFN_TM))

    w_in = o_w_in[0]
    c_kr = Q_LORA + KV_LORA
    c_u = c_kr + QK_ROPE
    w_lora = w_in[:, :c_kr].astype(BF16)
    w_kr = jnp.pad(w_in[:, c_kr:c_u], ((0, 0), (0, LANES - QK_ROPE))).astype(BF16)
    w_sgu = w_in[:, c_u:].astype(BF16)
    wq = o_w_qb[0].reshape(Q_LORA, MLA_HEADS, QK_NOPE + QK_ROPE)
    wq = jnp.pad(wq, ((0, 0), (0, 0), (0, HEAD_PAD - QK_NOPE - QK_ROPE)))
    wq = wq.reshape(Q_LORA, MLA_HEADS * HEAD_PAD).astype(BF16)
    wkv = o_w_kvb[0].reshape(KV_LORA, MLA_HEADS, QK_NOPE + V_HEAD)
    wkv = jnp.concatenate([wkv[:, :, :QK_NOPE].reshape(KV_LORA, -1),
                           wkv[:, :, QK_NOPE:].reshape(KV_LORA, -1)], axis=1).astype(BF16)
    cos_t, sin_t = _rope_tables(seq, MM_TILE)
    d_sgu = o_sgu_ln_g.shape[1]
    sgu_bias = jnp.repeat(o_sgu_b[0].T, d_sgu // SGU_HEADS, axis=1)
    q, k, v, sg = _mla_proj(x2, mods, mod_index(1, MM_TILE), norm1_g[1], w_lora, w_kr, w_sgu,
                            o_q_norm_g[0], wq, o_kv_norm_g[0], wkv, cos_t, sin_t,
                            o_sgu_ln_g[0], o_sgu_ln_b[0], o_sgu_w[0].astype(BF16), sgu_bias,
                            n_lat, seq)
    experts = (o_exp_w_gate[0], o_exp_w_up[0], o_exp_w_down[0])
    attn, experts_bf16 = _attention(q, k, v, n_batch, seq, ctx_len,
                                    [w.reshape(-1, w.shape[-1]) for w in experts])
    wg, wu, wd = (wb.reshape(w.shape) for wb, w in zip(experts_bf16, experts))

    rw = jnp.pad(o_router_w[0], ((0, 0), (0, LANES - N_EXPERTS))).astype(BF16)
    rb = jnp.concatenate([o_router_b[0], jnp.full((LANES - N_EXPERTS,), NEG, F32)]).reshape(1, LANES)
    x3, h3, r = _outproj([attn, sg], o_w_out[0].astype(BF16), (x2,), mods, mod_index(1, MM_TILE),
                         norm2_g[1], n_lat, F32, router=(rw, rb))

    te, n_active, tile_rows, route = _route_plan(r, n_lat, MOE_TM)
    ys = _moe(h3, wg, wu, wd, te, n_active, tile_rows, route, 2 * n_lat + MOE_TM)
    out = _combine(x3, ys, r, mods, mod_index(1, MM_TILE), final_norm_g)
    return out.reshape(n_batch, seq, d)
```

```python
import functools

import jax
import jax.numpy as jnp
import numpy as np
from jax import lax
from jax.experimental import pallas as pl
from jax.experimental.pallas import tpu as pltpu

F32 = jnp.float32
BF16 = jnp.bfloat16

EPS = 1e-6
N_MOD = 6
GRID_W = 64
POOL_WINDOWS = (2, 4, 8, 16)
CONV_WIDTH = 31
MLA_HEADS = 8
Q_LORA = 512
KV_LORA = 512
QK_NOPE = 128
QK_ROPE = 64
V_HEAD = 128
ROPE_FREQS = QK_ROPE // 4
ROPE_THETA = 10000.0
SM_SCALE = (QK_NOPE + QK_ROPE) ** -0.5
LOG2E = 1.4426950408889634
SGU_HEADS = 8
CHUNK = 128
N_EXPERTS = 8

LANES = 128
SUBLANES = 8
BF16_ROWS = 16
HALO = 16
VMEM_LIMIT = 56 * 1024 * 1024
HEAD_PAD = 256
ROW_TILE = 256
MM_TILE = 512
FFN_TM = 1024
FFN_TF = 512
MOE_TM = 512
MOE_NF = 7
MOE_ARMS = 4
ATTN_HEADS_PER_STEP = 2
NEG = -1e30


def _params(sem):
    return pltpu.CompilerParams(dimension_semantics=sem, vmem_limit_bytes=VMEM_LIMIT)


def _resident(shape):
    nd = len(shape)
    return pl.BlockSpec(shape, lambda *_: (0,) * nd, pipeline_mode=pl.Buffered(1))


def _slab_specs(arrays, n_steps, step_of):
    specs = []
    for w in arrays:
        rows, cols = w.shape
        n = max(k for k in range(1, n_steps + 1) if rows % (BF16_ROWS * k) == 0)
        specs.append(pl.BlockSpec((rows // n, cols),
                                  lambda *idx, n=n: (jnp.minimum(step_of(*idx), n - 1), 0)))
    return specs


def _cast_slabs(in_refs, out_refs):
    for w_ref, wb_ref in zip(in_refs, out_refs):
        wb_ref[...] = w_ref[...].astype(BF16)


def _rmsnorm_mod(x, g, shift, scale):
    ms = jnp.mean(x * x, axis=-1, keepdims=True)
    return (x * lax.rsqrt(ms + EPS) * g) * (1.0 + scale) + shift


def _layernorm(x, g, b):
    mu = jnp.mean(x, axis=-1, keepdims=True)
    xc = x - mu
    var = jnp.mean(xc * xc, axis=-1, keepdims=True)
    return xc * lax.rsqrt(var + EPS) * g + b


def _silu(x):
    return x * jax.nn.sigmoid(x)


def _ada_kernel(c_ref, w_ref, b_ref, o_ref):
    s = _silu(c_ref[...]).astype(BF16)
    o_ref[0] = jnp.dot(s, w_ref[0].astype(BF16), preferred_element_type=F32) + b_ref[0]


def _ada(cond, ada_w, ada_b, tn=1024):
    depth, d, n = ada_w.shape
    rows = cond.shape[0]
    return pl.pallas_call(
        _ada_kernel,
        grid=(depth, n // tn),
        in_specs=[pl.BlockSpec((rows, d), lambda l, j: (0, 0)),
                  pl.BlockSpec((1, d, tn), lambda l, j: (l, 0, j)),
                  pl.BlockSpec((1, 1, tn), lambda l, j: (l, 0, j))],
        out_specs=pl.BlockSpec((1, rows, tn), lambda l, j: (l, 0, j)),
        out_shape=jax.ShapeDtypeStruct((depth, rows, n), F32),
        compiler_params=_params(("arbitrary", "arbitrary")),
        name="ada",
    )(cond, ada_w, ada_b.reshape(depth, 1, n))


def _mod_spec(layer, tm, n_lat_rows, seq, n_batch, rows_per_layer):
    def index(i, *_):
        r = jnp.where(i * tm < n_lat_rows, (i * tm) // seq, n_batch)
        return (layer * rows_per_layer + r, 0, 0)
    return index


def _pick_rows(i, n_lat_tiles, lat_ref, ctx_ref):
    return jnp.where(i < n_lat_tiles, lat_ref[...], ctx_ref[...])


def _two_source_specs(tm, d, n_lat_tiles):
    return [pl.BlockSpec((tm, d), lambda i: (jnp.minimum(i, n_lat_tiles - 1), 0)),
            pl.BlockSpec((tm, d), lambda i: (jnp.maximum(i - n_lat_tiles, 0), 0))]


def _inproj_kernel(*refs, n_lat_tiles, n_cast):
    xl_ref, xc_ref, mod_ref, g_ref, w_ref = refs[:5]
    o_ref = refs[5 + n_cast]
    x = _pick_rows(pl.program_id(0), n_lat_tiles, xl_ref, xc_ref)
    h = _rmsnorm_mod(x, g_ref[...], mod_ref[0, 0:1, :], mod_ref[0, 1:2, :])
    o_ref[...] = jnp.dot(h.astype(BF16), w_ref[...], preferred_element_type=F32).astype(o_ref.dtype)
    _cast_slabs(refs[5:5 + n_cast], refs[6 + n_cast:])


def _inproj(x_lat, x_ctx, mods, mod_index, g, w, casts, tm=MM_TILE):
    d = x_lat.shape[1]
    t = x_lat.shape[0] + x_ctx.shape[0]
    n = w.shape[1]
    n_lat_tiles = x_lat.shape[0] // tm
    cast_specs = _slab_specs(casts, t // tm, lambda i: i)
    outs = pl.pallas_call(
        functools.partial(_inproj_kernel, n_lat_tiles=n_lat_tiles, n_cast=len(casts)),
        grid=(t // tm,),
        in_specs=_two_source_specs(tm, d, n_lat_tiles) + [
            pl.BlockSpec((1, N_MOD, d), mod_index), _resident((1, d)), _resident((d, n))] + cast_specs,
        out_specs=[pl.BlockSpec((tm, n), lambda i: (i, 0))] + cast_specs,
        out_shape=[jax.ShapeDtypeStruct((t, n), BF16)] + [jax.ShapeDtypeStruct(c.shape, BF16) for c in casts],
        compiler_params=_params(("arbitrary",)),
        name="inproj0",
    )(x_lat, x_ctx, mods, g.reshape(1, d), w, *casts)
    return outs[0], outs[1:]


def _poolconv_kernel(up_ref, um_ref, un_ref, ap_ref, am_ref, an_ref, gp_ref, gm_ref, gn_ref,
                     pw_ref, ps_ref, cw_ref, cb_ref, lg_ref, lb_ref, y_ref,
                     ubuf, zbuf, zsh, cbuf, *, n_lat_tiles, seq, ctx_len):
    i = pl.program_id(0)
    tm = um_ref.shape[0]
    dp = um_ref.shape[1]
    is_lat = i < n_lat_tiles
    pos0 = jnp.where(is_lat, (i * tm) % seq, ((i - n_lat_tiles) * tm) % ctx_len)
    length = jnp.where(is_lat, seq, ctx_len)
    keep_p = jnp.where(pos0 == 0, 0.0, 1.0).astype(F32)
    keep_n = jnp.where(pos0 + tm == length, 0.0, 1.0).astype(F32)

    def glu(a_ref, g_ref):
        return a_ref[...].astype(F32) * jax.nn.sigmoid(g_ref[...].astype(F32))

    ubuf[0:HALO, :] = up_ref[...].astype(F32) * keep_p
    ubuf[HALO:HALO + tm, :] = um_ref[...].astype(F32)
    ubuf[HALO + tm:, :] = un_ref[...].astype(F32) * keep_n
    zbuf[0:HALO, :] = glu(ap_ref, gp_ref) * keep_p
    zbuf[HALO:HALO + tm, :] = glu(am_ref, gm_ref)
    zbuf[HALO + tm:, :] = glu(an_ref, gn_ref) * keep_n

    pos = pos0 + lax.broadcasted_iota(jnp.int32, (tm, 1), 0)
    gdim = dp // len(POOL_WINDOWS)
    for g, w in enumerate(POOL_WINDOWS):
        cols = slice(g * gdim, (g + 1) * gdim)
        s = ubuf[HALO - w // 2:HALO - w // 2 + tm, cols]
        for o in range(-w // 2 + 1, w // 2):
            s = s + ubuf[HALO + o:HALO + o + tm, cols]
        cnt = jnp.minimum(pos - w // 2 + w, length) - jnp.maximum(pos - w // 2, 0)
        pooled = s / cnt.astype(F32) - ubuf[HALO:HALO + tm, cols]
        mixed = jnp.dot(pooled.astype(BF16), pw_ref[g], preferred_element_type=F32)
        y_ref[:, cols] = (mixed * ps_ref[:, cols]).astype(y_ref.dtype)

    n_sh = zsh.shape[1]
    for j in range(1, SUBLANES):
        zsh[j - 1] = zbuf[j:j + n_sh, :]
    rb = 64
    first = HALO - CONV_WIDTH // 2
    for c in range(0, zbuf.shape[1], LANES):
        cols = slice(c, c + LANES)
        taps = [cw_ref[k:k + 1, cols] for k in range(CONV_WIDTH)]
        for r in range(0, tm, rb):
            acc = None
            for k in range(CONV_WIDTH):
                q, j = divmod(first + k, SUBLANES)
                rows = slice(q * SUBLANES + r, q * SUBLANES + r + rb)
                z = zbuf[rows, cols] if j == 0 else zsh[j - 1, rows, cols]
                acc = taps[k] * z if acc is None else acc + taps[k] * z
            cbuf[r:r + rb, cols] = acc
    conv = _layernorm(cbuf[...] + cb_ref[...], lg_ref[...], lb_ref[...])
    y_ref[:, dp:] = _silu(conv).astype(y_ref.dtype)


def _poolconv(p, pool_w, pool_scale, conv_w, conv_b, ln_g, ln_b, n_lat_rows, seq, ctx_len,
              tm=ROW_TILE):
    t = p.shape[0]
    dp = pool_scale.shape[0]
    dc = conv_b.shape[0]
    hb = tm // HALO
    last_hb = t // HALO - 1

    def main(c):
        return pl.BlockSpec((tm, dp), lambda i: (i, c))

    def prev(c):
        return pl.BlockSpec((HALO, dp), lambda i: (jnp.maximum(i * hb - 1, 0), c))

    def nxt(c):
        return pl.BlockSpec((HALO, dp), lambda i: (jnp.minimum((i + 1) * hb, last_hb), c))

    kern = functools.partial(_poolconv_kernel, n_lat_tiles=n_lat_rows // tm, seq=seq, ctx_len=ctx_len)
    return pl.pallas_call(
        kern,
        grid=(t // tm,),
        in_specs=[prev(0), main(0), nxt(0), prev(1), main(1), nxt(1), prev(2), main(2), nxt(2),
                  _resident(pool_w.shape), _resident((1, dp)), _resident(conv_w.shape),
                  _resident((1, dc)), _resident((1, dc)), _resident((1, dc))],
        out_specs=pl.BlockSpec((tm, dp + dc), lambda i: (i, 0)),
        out_shape=jax.ShapeDtypeStruct((t, dp + dc), BF16),
        scratch_shapes=[pltpu.VMEM((tm + 2 * HALO, dp), F32),
                        pltpu.VMEM((tm + 2 * HALO, dc), F32),
                        pltpu.VMEM((SUBLANES - 1, tm + 2 * HALO - SUBLANES, dc), F32),
                        pltpu.VMEM((tm, dc), F32)],
        compiler_params=_params(("arbitrary",)),
        name="poolconv",
    )(p, p, p, p, p, p, p, p, p, pool_w, pool_scale.reshape(1, dp), conv_w,
      conv_b.reshape(1, dc), ln_g.reshape(1, dc), ln_b.reshape(1, dc))


def _outproj_kernel(*refs, n_in, n_res, n_lat_tiles, router):
    y_refs = refs[:n_in]
    w_ref = refs[n_in]
    x_refs = refs[n_in + 1:n_in + 1 + n_res]
    mod_ref, g_ref = refs[n_in + 1 + n_res:n_in + 3 + n_res]
    rest = refs[n_in + 3 + n_res:]
    if router:
        rw_ref, rb_ref, xo_ref, ho_ref, r_ref = rest
    else:
        xo_ref, ho_ref = rest
    o = None
    k0 = 0
    for y_ref in y_refs:
        kk = y_ref.shape[1]
        part = jnp.dot(y_ref[...], w_ref[k0:k0 + kk, :], preferred_element_type=F32)
        o = part if o is None else o + part
        k0 += kk
    if n_res == 2:
        x = _pick_rows(pl.program_id(0), n_lat_tiles, x_refs[0], x_refs[1])
    else:
        x = x_refs[0][...]
    x1 = x + mod_ref[0, 2:3, :] * o
    xo_ref[...] = x1
    h = _rmsnorm_mod(x1, g_ref[...], mod_ref[0, 3:4, :], mod_ref[0, 4:5, :])
    ho_ref[...] = h.astype(ho_ref.dtype)
    if router:
        logits = jnp.dot(h.astype(BF16), rw_ref[...], preferred_element_type=F32) + rb_ref[...]
        lane = lax.broadcasted_iota(jnp.int32, logits.shape, 1)
        m1 = jnp.max(logits, axis=-1, keepdims=True)
        i1 = jnp.min(jnp.where(logits == m1, lane, LANES), axis=-1, keepdims=True)
        rest_l = jnp.where(lane == i1, NEG * 2, logits)
        m2 = jnp.max(rest_l, axis=-1, keepdims=True)
        i2 = jnp.min(jnp.where(rest_l == m2, lane, LANES), axis=-1, keepdims=True)
        e2 = jnp.exp(m2 - m1)
        p1 = 1.0 / (1.0 + e2)
        p2 = e2 / (1.0 + e2)
        r = jnp.where(lane == 0, i1.astype(F32), 0.0)
        r = jnp.where(lane == 1, i2.astype(F32), r)
        r = jnp.where(lane == 2, p1, r)
        r = jnp.where(lane == 3, p2, r)
        r_ref[...] = r


def _outproj(ys, w, res, mods, mod_index, g, n_rows, h_dtype, router=None, tm=MM_TILE):
    d = w.shape[1]
    in_specs = [pl.BlockSpec((tm, y.shape[1]), lambda i: (i, 0)) for y in ys]
    in_specs.append(_resident(w.shape))
    n_lat_tiles = res[0].shape[0] // tm
    if len(res) == 2:
        in_specs += _two_source_specs(tm, d, n_lat_tiles)
    else:
        in_specs.append(pl.BlockSpec((tm, d), lambda i: (i, 0)))
    in_specs += [pl.BlockSpec((1, N_MOD, d), mod_index), _resident((1, d))]
    args = list(ys) + [w] + list(res) + [mods, g.reshape(1, d)]
    out_specs = [pl.BlockSpec((tm, d), lambda i: (i, 0)), pl.BlockSpec((tm, d), lambda i: (i, 0))]
    out_shape = [jax.ShapeDtypeStruct((n_rows, d), F32), jax.ShapeDtypeStruct((n_rows, d), h_dtype)]
    if router is not None:
        rw, rb = router
        in_specs += [_resident(rw.shape), _resident(rb.shape)]
        args += [rw, rb]
        out_specs.append(pl.BlockSpec((tm, LANES), lambda i: (i, 0)))
        out_shape.append(jax.ShapeDtypeStruct((n_rows, LANES), F32))
    kern = functools.partial(_outproj_kernel, n_in=len(ys), n_res=len(res), n_lat_tiles=n_lat_tiles,
                             router=router is not None)
    return pl.pallas_call(
        kern,
        grid=(n_rows // tm,),
        in_specs=in_specs,
        out_specs=out_specs,
        out_shape=out_shape,
        compiler_params=_params(("arbitrary",)),
        name="outproj_router" if router is not None else "outproj",
    )(*args)


def _ffn_kernel(h_ref, wg_ref, wu_ref, wd_ref, x_hbm, mod_ref, o_ref, xres, sem):
    i = pl.program_id(0)
    f = pl.program_id(1)
    tm = o_ref.shape[0]
    residual = pltpu.make_async_copy(x_hbm.at[pl.ds(pl.multiple_of(i * tm, tm), tm)], xres, sem)

    @pl.when(f == 0)
    def _():
        residual.start()
        o_ref[...] = jnp.zeros(o_ref.shape, F32)

    h = h_ref[...]
    a = jnp.dot(h, wg_ref[...], preferred_element_type=F32)
    b = jnp.dot(h, wu_ref[...], preferred_element_type=F32)
    o_ref[...] += jnp.dot((_silu(a) * b).astype(BF16), wd_ref[...], preferred_element_type=F32)

    @pl.when(f == pl.num_programs(1) - 1)
    def _():
        residual.wait()
        o_ref[...] = xres[...] + mod_ref[0, 5:6, :] * o_ref[...]


def _ffn(h, wg, wu, wd, xs, mods, mod_index, tm=FFN_TM, tf=FFN_TF):
    t, d = xs.shape
    ff = wg.shape[1]
    return pl.pallas_call(
        _ffn_kernel,
        grid=(t // tm, ff // tf),
        in_specs=[pl.BlockSpec((tm, d), lambda i, f: (i, 0)),
                  pl.BlockSpec((d, tf), lambda i, f: (0, f)),
                  pl.BlockSpec((d, tf), lambda i, f: (0, f)),
                  pl.BlockSpec((tf, d), lambda i, f: (f, 0)),
                  pl.BlockSpec(memory_space=pl.ANY),
                  pl.BlockSpec((1, N_MOD, d), mod_index)],
        out_specs=pl.BlockSpec((tm, d), lambda i, f: (i, 0)),
        out_shape=jax.ShapeDtypeStruct((t, d), F32),
        scratch_shapes=[pltpu.VMEM((tm, d), F32), pltpu.SemaphoreType.DMA],
        compiler_params=_params(("arbitrary", "arbitrary")),
        name="ffn",
    )(h, wg, wu, wd, xs, mods)


def _rope(x, c, s):
    lane = lax.broadcasted_iota(jnp.int32, x.shape, 1)
    partner = jnp.where(lane % 32 < 16, pltpu.roll(x, LANES - 16, 1), pltpu.roll(x, 16, 1))
    return x * c + partner * s


def _mla_kernel(x_ref, mod_ref, g_ref, wlora_ref, wkr_ref, wsgu_ref, qg_ref, wq_ref, kvg_ref, wkv_ref,
                cos_ref, sin_ref, slg_ref, slb_ref, sw_ref, sb_ref,
                q_ref, k_ref, v_ref, sg_ref, *, n_lat_tiles):
    i = pl.program_id(0)
    tm = x_ref.shape[0]
    h = _rmsnorm_mod(x_ref[...], g_ref[...], mod_ref[0, 0:1, :], mod_ref[0, 1:2, :]).astype(BF16)
    p_lora = jnp.dot(h, wlora_ref[...], preferred_element_type=F32)
    p_sgu = jnp.dot(h, wsgu_ref[...], preferred_element_type=F32)
    d_sgu = slg_ref.shape[1]
    cos = cos_ref[...]
    sin = sin_ref[...]

    def rms(x, g):
        return x * lax.rsqrt(jnp.mean(x * x, axis=-1, keepdims=True) + EPS) * g

    ckv = rms(p_lora[:, Q_LORA:], kvg_ref[...]).astype(BF16)
    kv = jnp.dot(ckv, wkv_ref[...], preferred_element_type=F32)
    kr = _rope(jnp.dot(h, wkr_ref[...], preferred_element_type=F32), cos, sin).astype(BF16)
    nk = MLA_HEADS * QK_NOPE
    for hd in range(MLA_HEADS):
        k_ref[:, hd * HEAD_PAD:hd * HEAD_PAD + QK_NOPE] = kv[:, hd * QK_NOPE:(hd + 1) * QK_NOPE].astype(BF16)
        k_ref[:, hd * HEAD_PAD + QK_NOPE:(hd + 1) * HEAD_PAD] = kr
        v_ref[:, hd * HEAD_PAD:hd * HEAD_PAD + V_HEAD] = kv[:, nk + hd * V_HEAD:nk + (hd + 1) * V_HEAD].astype(BF16)
        v_ref[:, hd * HEAD_PAD + V_HEAD:(hd + 1) * HEAD_PAD] = jnp.ones((tm, HEAD_PAD - V_HEAD), BF16)

    @pl.when(i < n_lat_tiles)
    def _():
        cq = rms(p_lora[:, :Q_LORA], qg_ref[...]).astype(BF16)
        q = jnp.dot(cq, wq_ref[...], preferred_element_type=F32) * (SM_SCALE * LOG2E)
        for hd in range(MLA_HEADS):
            b0 = hd * HEAD_PAD
            q_ref[:, b0:b0 + QK_NOPE] = q[:, b0:b0 + QK_NOPE].astype(BF16)
            q_ref[:, b0 + QK_NOPE:b0 + HEAD_PAD] = _rope(q[:, b0 + QK_NOPE:b0 + HEAD_PAD], cos, sin).astype(BF16)

        zu = jax.nn.gelu(p_sgu[:, :d_sgu])
        zv = _layernorm(jax.nn.gelu(p_sgu[:, d_sgu:]), slg_ref[...], slb_ref[...]).astype(BF16)
        hdim = d_sgu // SGU_HEADS
        for ck in range(tm // CHUNK):
            rows = slice(ck * CHUNK, (ck + 1) * CHUNK)
            for hd in range(SGU_HEADS):
                cols = slice(hd * hdim, (hd + 1) * hdim)
                mixed = jnp.dot(sw_ref[hd], zv[rows, cols], preferred_element_type=F32) + sb_ref[:, cols]
                sg_ref[rows, cols] = (zu[rows, cols] * mixed).astype(BF16)


def _mla_proj(xs, mods, mod_index, g, w_lora, w_kr, w_sgu, qg, wq, kvg, wkv, cos_t, sin_t, slg, slb, sw, sb,
              n_lat_rows, seq, tm=MM_TILE):
    t, d = xs.shape
    n_lat_tiles = n_lat_rows // tm
    seq_tiles = seq // tm
    d_sgu = slg.shape[0]

    def lat(i):
        return (jnp.minimum(i, n_lat_tiles - 1), 0)

    def table(i):
        return (jnp.where(i < n_lat_tiles, i % seq_tiles, seq_tiles), 0)

    kern = functools.partial(_mla_kernel, n_lat_tiles=n_lat_tiles)
    return pl.pallas_call(
        kern,
        grid=(t // tm,),
        in_specs=[pl.BlockSpec((tm, d), lambda i: (i, 0)),
                  pl.BlockSpec((1, N_MOD, d), mod_index),
                  _resident((1, d)), _resident(w_lora.shape), _resident(w_kr.shape), _resident(w_sgu.shape),
                  _resident((1, Q_LORA)), _resident(wq.shape),
                  _resident((1, KV_LORA)), _resident(wkv.shape),
                  pl.BlockSpec((tm, LANES), table), pl.BlockSpec((tm, LANES), table),
                  _resident((1, d_sgu)), _resident((1, d_sgu)), _resident(sw.shape), _resident(sb.shape)],
        out_specs=[pl.BlockSpec((tm, MLA_HEADS * HEAD_PAD), lat),
                   pl.BlockSpec((tm, MLA_HEADS * HEAD_PAD), lambda i: (i, 0)),
                   pl.BlockSpec((tm, MLA_HEADS * HEAD_PAD), lambda i: (i, 0)),
                   pl.BlockSpec((tm, d_sgu), lat)],
        out_shape=[jax.ShapeDtypeStruct((n_lat_rows, MLA_HEADS * HEAD_PAD), BF16),
                   jax.ShapeDtypeStruct((t, MLA_HEADS * HEAD_PAD), BF16),
                   jax.ShapeDtypeStruct((t, MLA_HEADS * HEAD_PAD), BF16),
                   jax.ShapeDtypeStruct((n_lat_rows, d_sgu), BF16)],
        compiler_params=_params(("arbitrary",)),
        name="mla_proj",
    )(xs, mods, g.reshape(1, d), w_lora, w_kr, w_sgu, qg.reshape(1, -1), wq, kvg.reshape(1, -1), wkv,
      cos_t, sin_t, slg.reshape(1, -1), slb.reshape(1, -1), sw, sb)


def _attn_kernel(*refs, n_cast):
    q_ref, kc_ref, kl_ref, vc_ref, vl_ref = refs[:5]
    cast_in = refs[5:5 + n_cast]
    o_ref = refs[5 + n_cast]
    cast_out = refs[6 + n_cast:]
    dn = (((1,), (1,)), ((), ()))

    def scores(hd):
        qk = slice(hd * HEAD_PAD, (hd + 1) * HEAD_PAD)
        q = q_ref[:, qk]
        return (lax.dot_general(q, kc_ref[:, qk], dn, preferred_element_type=F32),
                lax.dot_general(q, kl_ref[:, qk], dn, preferred_element_type=F32))

    def finish(hd, sc, sl):
        vv = slice(hd * HEAD_PAD, (hd + 1) * HEAD_PAD)
        m = jnp.maximum(jnp.max(sc, axis=-1, keepdims=True), jnp.max(sl, axis=-1, keepdims=True))
        pc = jnp.exp2((sc - m).astype(BF16))
        pq = jnp.exp2((sl - m).astype(BF16))
        o = (jnp.dot(pc, vc_ref[:, vv], preferred_element_type=F32)
             + jnp.dot(pq, vl_ref[:, vv], preferred_element_type=F32))
        out = slice(hd * V_HEAD, (hd + 1) * V_HEAD)
        o_ref[:, out] = (o[:, :V_HEAD] / o[:, V_HEAD:V_HEAD + 1]).astype(o_ref.dtype)

    s_next = scores(0)
    for hd in range(ATTN_HEADS_PER_STEP):
        s_cur = s_next
        if hd + 1 < ATTN_HEADS_PER_STEP:
            s_next = scores(hd + 1)
        finish(hd, *s_cur)
    _cast_slabs(cast_in, cast_out)


def _attention(q, k, v, n_batch, seq, ctx_len, casts, tq=ROW_TILE):
    n_lat_rows = n_batch * seq
    qb = seq // tq
    ctx_blk0 = n_lat_rows // ctx_len
    hs = ATTN_HEADS_PER_STEP
    hp = MLA_HEADS // hs
    n_steps = n_batch * hp * qb

    cast_specs = _slab_specs(casts, n_steps, lambda b, h, j: (b * hp + h) * qb + j)
    outs = pl.pallas_call(
        functools.partial(_attn_kernel, n_cast=len(casts)),
        grid=(n_batch, hp, qb),
        in_specs=[pl.BlockSpec((tq, hs * HEAD_PAD), lambda b, h, j: (b * qb + j, h)),
                  pl.BlockSpec((ctx_len, hs * HEAD_PAD), lambda b, h, j: (ctx_blk0 + b, h)),
                  pl.BlockSpec((seq, hs * HEAD_PAD), lambda b, h, j: (b, h)),
                  pl.BlockSpec((ctx_len, hs * HEAD_PAD), lambda b, h, j: (ctx_blk0 + b, h)),
                  pl.BlockSpec((seq, hs * HEAD_PAD), lambda b, h, j: (b, h))] + cast_specs,
        out_specs=[pl.BlockSpec((tq, hs * V_HEAD), lambda b, h, j: (b * qb + j, h))] + cast_specs,
        out_shape=[jax.ShapeDtypeStruct((n_lat_rows, MLA_HEADS * V_HEAD), BF16)]
        + [jax.ShapeDtypeStruct(w.shape, BF16) for w in casts],
        compiler_params=_params(("arbitrary", "arbitrary", "arbitrary")),
        name="attention",
    )(q, k, k, v, v, *casts)
    return outs[0], outs[1:]


SRC_BITS = 15
ITEM_RANGE = 1 << 16


def _moe_kernel(te_ref, nact_ref, rows_ref, route_ref, h_hbm, wg_ref, wu_ref, wd_ref, y_hbm,
                xbuf, xb, acc, gsem, ssem):
    i = pl.program_id(0)
    f = pl.program_id(1)
    tm = xbuf.shape[0]
    chunk = tm // (MOE_NF + 1)
    nact = nact_ref[0]
    active = i < nact
    slot = i % 2
    dump0 = y_hbm.shape[0] - tm

    def gather_copy(tile, r):
        src = route_ref[(tile + 1) * tm + r] & ((1 << SRC_BITS) - 1)
        return pltpu.make_async_copy(h_hbm.at[pl.ds(src, 1)], xbuf.at[pl.ds(r, 1)], gsem)

    def scatter_copy(tile, r):
        dst = route_ref[(tile + 1) * tm + r] >> SRC_BITS
        return pltpu.make_async_copy(acc.at[tile % 2, pl.ds(r, 1)], y_hbm.at[pl.ds(dst, 1)], ssem)

    def wait_gather():
        pltpu.make_async_copy(h_hbm.at[pl.ds(0, tm)], xbuf, gsem).wait()

    def wait_scatter():
        pltpu.make_async_copy(acc.at[0], y_hbm.at[pl.ds(0, tm)], ssem).wait()

    def start_all(copy, tile):
        def body(r, c):
            copy(tile, r).start()
            return c
        lax.fori_loop(0, tm, body, 0)

    @pl.when((i == 0) & (f == 0))
    def _():
        acc[1] = jnp.zeros(acc.shape[1:], F32)
        fill = pltpu.make_async_copy(acc.at[1], y_hbm.at[pl.ds(dump0, tm)], ssem)
        fill.start()
        fill.wait()
        start_all(gather_copy, 0)

    @pl.when((f == 0) & (i <= nact))
    def _():
        wait_gather()

        @pl.when(i >= 1)
        def _():
            wait_scatter()

    def issue_chunk(c):
        for j in range(chunk):
            r = c * chunk + j
            gather_copy(i + 1, r).start()
            scatter_copy(i - 1, r).start()

    @pl.when((f == 0) & active)
    def _():
        xb[...] = xbuf[...].astype(BF16)
        acc[slot] = jnp.zeros(acc.shape[1:], F32)
        issue_chunk(0)

    @pl.when((f == 0) & (i == nact) & (i >= 1))
    def _():
        start_all(scatter_copy, i - 1)
        wait_scatter()

    def expert_step(m):
        issue_chunk(f + 1)
        x = xb[0:m, :]
        a = jnp.dot(x, wg_ref[0], preferred_element_type=F32)
        b = jnp.dot(x, wu_ref[0], preferred_element_type=F32)
        acc[slot, 0:m, :] += jnp.dot((_silu(a) * b).astype(BF16), wd_ref[0], preferred_element_type=F32)

    part = tm // MOE_ARMS
    filled = (rows_ref[i] + part - 1) // part
    for arm in range(1, MOE_ARMS + 1):
        pl.when(active & (filled == arm))(functools.partial(expert_step, arm * part))


def _moe(h, wg, wu, wd, tile_expert, n_active, tile_rows, route, n_out_rows, tm=MOE_TM):
    d = h.shape[1]
    ff = wg.shape[2]
    n_tiles = tile_expert.shape[0]
    nf = MOE_NF
    tf = ff // nf

    def fidx(i, f, nact):
        return jnp.where(i < nact[0], f, nf - 1)

    grid_spec = pltpu.PrefetchScalarGridSpec(
        num_scalar_prefetch=4,
        grid=(n_tiles, nf),
        in_specs=[pl.BlockSpec(memory_space=pl.ANY),
                  pl.BlockSpec((1, d, tf), lambda i, f, te, nact, rows, route: (te[i], 0, fidx(i, f, nact))),
                  pl.BlockSpec((1, d, tf), lambda i, f, te, nact, rows, route: (te[i], 0, fidx(i, f, nact))),
                  pl.BlockSpec((1, tf, d), lambda i, f, te, nact, rows, route: (te[i], fidx(i, f, nact), 0))],
        out_specs=pl.BlockSpec(memory_space=pl.ANY),
        scratch_shapes=[pltpu.VMEM((tm, d), F32), pltpu.VMEM((tm, d), BF16), pltpu.VMEM((2, tm, d), F32),
                        pltpu.SemaphoreType.DMA, pltpu.SemaphoreType.DMA],
    )
    return pl.pallas_call(
        _moe_kernel,
        grid_spec=grid_spec,
        out_shape=jax.ShapeDtypeStruct((n_out_rows, d), F32),
        compiler_params=_params(("arbitrary", "arbitrary")),
        name="moe",
    )(tile_expert, n_active, tile_rows, route, h, wg, wu, wd)


def _route_plan(r, n_tok, tm):
    n_assign = 2 * n_tok
    n_tiles = n_assign // tm + N_EXPERTS
    assert n_tiles * tm <= ITEM_RANGE
    e_flat = jnp.concatenate([r[:, 0], r[:, 1]]).astype(jnp.int32)
    experts = jnp.arange(N_EXPERTS, dtype=jnp.int32)
    counts = jnp.sum(e_flat[:, None] == experts[None, :], axis=0).astype(jnp.int32)
    tiles_per = (counts + tm - 1) // tm
    tend = jnp.cumsum(tiles_per)
    n_active = tend[-1]
    j = jnp.arange(n_tiles, dtype=jnp.int32)
    te = jnp.minimum(jnp.sum(j[:, None] >= tend[None, :], axis=1), N_EXPERTS - 1).astype(jnp.int32)
    last = jnp.sum(jnp.where(j == n_active - 1, te, 0))
    te = jnp.where(j < n_active, te, last)
    mine = te[:, None] == experts[None, :]
    first_tile = jnp.sum(jnp.where(mine, (tend - tiles_per)[None, :], 0), axis=1)
    group_rows = jnp.sum(jnp.where(mine, counts[None, :], 0), axis=1)
    tile_rows = jnp.where(j < n_active, jnp.clip(group_rows - (j - first_tile) * tm, 0, tm), 0).astype(jnp.int32)
    pad_id = jnp.arange(tm, dtype=jnp.int32)
    pad_on = pad_id[None, :] < (tiles_per * tm - counts)[:, None]
    pad_key = jnp.where(pad_on, 2 * experts[:, None] + 1, 2 * N_EXPERTS).reshape(-1)
    keys = jnp.concatenate([2 * e_flat, pad_key])
    row = jnp.arange(n_tiles * tm, dtype=jnp.int32)
    item = jnp.sort(keys * ITEM_RANGE + row) % ITEM_RANGE
    real = item < n_assign
    src = jnp.where(real, item % n_tok, 0)
    dst = jnp.where(real, item, n_assign + row % tm)
    body = src | (dst << SRC_BITS)
    edge = (jnp.arange(tm, dtype=jnp.int32) + n_assign) << SRC_BITS
    route = jnp.concatenate([edge, body, edge]).astype(jnp.int32)
    return te, n_active.reshape(1).astype(jnp.int32), tile_rows, route


def _combine_kernel(x_ref, y0_ref, y1_ref, r_ref, mod_ref, g_ref, o_ref):
    r = r_ref[...]
    y = r[:, 2:3] * y0_ref[...] + r[:, 3:4] * y1_ref[...]
    x = x_ref[...] + mod_ref[0, 5:6, :] * y
    ms = jnp.mean(x * x, axis=-1, keepdims=True)
    o_ref[...] = x * lax.rsqrt(ms + EPS) * g_ref[...]


def _combine(xs, y, r, mods, mod_index, g, tm=MM_TILE):
    t, d = xs.shape
    nb = t // tm
    return pl.pallas_call(
        _combine_kernel,
        grid=(nb,),
        in_specs=[pl.BlockSpec((tm, d), lambda i: (i, 0)),
                  pl.BlockSpec((tm, d), lambda i: (i, 0)),
                  pl.BlockSpec((tm, d), lambda i: (nb + i, 0)),
                  pl.BlockSpec((tm, LANES), lambda i: (i, 0)),
                  pl.BlockSpec((1, N_MOD, d), mod_index),
                  _resident((1, d))],
        out_specs=pl.BlockSpec((tm, d), lambda i: (i, 0)),
        out_shape=jax.ShapeDtypeStruct((t, d), F32),
        compiler_params=_params(("arbitrary",)),
        name="combine",
    )(xs, y, y, r, mods, g.reshape(1, d))


def _rope_tables(seq, extra_rows):
    rows = seq // GRID_W
    row = np.repeat(np.arange(rows), GRID_W).astype(np.float32)
    col = np.tile(np.arange(GRID_W), rows).astype(np.float32)
    inv = np.float32(ROPE_THETA) ** (-np.arange(ROPE_FREQS, dtype=np.float32) / np.float32(ROPE_FREQS))
    ar = row[:, None] * inv
    ac = col[:, None] * inv
    cos = np.ones((seq + extra_rows, LANES), np.float32)
    sin = np.zeros((seq + extra_rows, LANES), np.float32)
    cos[:seq, :QK_ROPE] = np.concatenate([np.cos(ar), np.cos(ar), np.cos(ac), np.cos(ac)], axis=1)
    sin[:seq, :QK_ROPE] = np.concatenate([-np.sin(ar), np.sin(ar), -np.sin(ac), np.sin(ac)], axis=1)
    return jnp.asarray(cos), jnp.asarray(sin)


def kernel(x, c, ctx, c_ctx, ada_w, ada_b, norm1_g, norm2_g, e_w_in, e_pool_w, e_pool_scale, e_conv_w, e_conv_b, e_conv_ln_g, e_conv_ln_b, e_w_out, e_ffn_w_gate, e_ffn_w_up, e_ffn_w_down, o_w_in, o_q_norm_g, o_w_qb, o_kv_norm_g, o_w_kvb, o_sgu_ln_g, o_sgu_ln_b, o_sgu_w, o_sgu_b, o_w_out, o_router_w, o_router_b, o_exp_w_gate, o_exp_w_up, o_exp_w_down, final_norm_g):
    n_batch, seq, d = x.shape
    ctx_len = ctx.shape[1]
    n_lat = n_batch * seq
    n_ctx = n_batch * ctx_len
    assert ctx_len % ROW_TILE == 0 and seq % FFN_TM == 0 and n_ctx % FFN_TM == 0
    assert ada_w.shape[0] == 2 and e_w_in.shape[0] == 1 and o_w_in.shape[0] == 1

    rows_per_layer = 8 * ((n_batch + 1 + 7) // 8)
    cond = jnp.zeros((rows_per_layer, d), F32).at[:n_batch].set(c).at[n_batch].set(c_ctx)
    mods = _ada(cond, ada_w, ada_b).reshape(2 * rows_per_layer, N_MOD, d)

    def mod_index(layer, tm):
        return _mod_spec(layer, tm, n_lat, seq, n_batch, rows_per_layer)

    x_lat = x.reshape(n_lat, d)
    x_ctx = ctx.reshape(n_ctx, d)

    p, (fg, fu, fd) = _inproj(x_lat, x_ctx, mods, mod_index(0, MM_TILE), norm1_g[0], e_w_in[0].astype(BF16),
                              [e_ffn_w_gate[0], e_ffn_w_up[0], e_ffn_w_down[0]])
    y = _poolconv(p, e_pool_w[0].astype(BF16), e_pool_scale[0], e_conv_w[0], e_conv_b[0],
                  e_conv_ln_g[0], e_conv_ln_b[0], n_lat, seq, ctx_len)
    x1, h2 = _outproj([y], e_w_out[0].astype(BF16), (x_lat, x_ctx), mods, mod_index(0, MM_TILE), norm2_g[0],
                      n_lat + n_ctx, BF16)
    x2 = _ffn(h2, fg, fu, fd, x1, mods, mod_index(0, FFN_TM))

    w_in = o_w_in[0]
    c_kr = Q_LORA + KV_LORA
    c_u = c_kr + QK_ROPE
    w_lora = w_in[:, :c_kr].astype(BF16)
    w_kr = jnp.pad(w_in[:, c_kr:c_u], ((0, 0), (0, LANES - QK_ROPE))).astype(BF16)
    w_sgu = w_in[:, c_u:].astype(BF16)
    wq = o_w_qb[0].reshape(Q_LORA, MLA_HEADS, QK_NOPE + QK_ROPE)
    wq = jnp.pad(wq, ((0, 0), (0, 0), (0, HEAD_PAD - QK_NOPE - QK_ROPE)))
    wq = wq.reshape(Q_LORA, MLA_HEADS * HEAD_PAD).astype(BF16)
    wkv = o_w_kvb[0].reshape(KV_LORA, MLA_HEADS, QK_NOPE + V_HEAD)
    wkv = jnp.concatenate([wkv[:, :, :QK_NOPE].reshape(KV_LORA, -1),
                           wkv[:, :, QK_NOPE:].reshape(KV_LORA, -1)], axis=1).astype(BF16)
    cos_t, sin_t = _rope_tables(seq, MM_TILE)
    d_sgu = o_sgu_ln_g.shape[1]
    sgu_bias = jnp.repeat(o_sgu_b[0].T, d_sgu // SGU_HEADS, axis=1)
    q, k, v, sg = _mla_proj(x2, mods, mod_index(1, MM_TILE), norm1_g[1], w_lora, w_kr, w_sgu,
                            o_q_norm_g[0], wq, o_kv_norm_g[0], wkv, cos_t, sin_t,
                            o_sgu_ln_g[0], o_sgu_ln_b[0], o_sgu_w[0].astype(BF16), sgu_bias,
                            n_lat, seq)
    experts = (o_exp_w_gate[0], o_exp_w_up[0], o_exp_w_down[0])
    attn, experts_bf16 = _attention(q, k, v, n_batch, seq, ctx_len,
                                    [w.reshape(-1, w.shape[-1]) for w in experts])
    wg, wu, wd = (wb.reshape(w.shape) for wb, w in zip(experts_bf16, experts))

    rw = jnp.pad(o_router_w[0], ((0, 0), (0, LANES - N_EXPERTS))).astype(BF16)
    rb = jnp.concatenate([o_router_b[0], jnp.full((LANES - N_EXPERTS,), NEG, F32)]).reshape(1, LANES)
    x3, h3, r = _outproj([attn, sg], o_w_out[0].astype(BF16), (x2,), mods, mod_index(1, MM_TILE),
                         norm2_g[1], n_lat, F32, router=(rw, rb))

    te, n_active, tile_rows, route = _route_plan(r, n_lat, MOE_TM)
    ys = _moe(h3, wg, wu, wd, te, n_active, tile_rows, route, 2 * n_lat + MOE_TM)
    out = _combine(x3, ys, r, mods, mod_index(1, MM_TILE), final_norm_g)
    return out.reshape(n_batch, seq, d)
```

```python
import functools

import jax
import jax.numpy as jnp
import numpy as np
from jax import lax
from jax.experimental import pallas as pl
from jax.experimental.pallas import tpu as pltpu

F32 = jnp.float32
BF16 = jnp.bfloat16

EPS = 1e-6
N_MOD = 6
GRID_W = 64
POOL_WINDOWS = (2, 4, 8, 16)
CONV_WIDTH = 31
MLA_HEADS = 8
Q_LORA = 512
KV_LORA = 512
QK_NOPE = 128
QK_ROPE = 64
V_HEAD = 128
ROPE_FREQS = QK_ROPE // 4
ROPE_THETA = 10000.0
SM_SCALE = (QK_NOPE + QK_ROPE) ** -0.5
LOG2E = 1.4426950408889634
SGU_HEADS = 8
CHUNK = 128
N_EXPERTS = 8

LANES = 128
SUBLANES = 8
BF16_ROWS = 16
HALO = 16
VMEM_LIMIT = 56 * 1024 * 1024
HEAD_PAD = 256
ROW_TILE = 256
MM_TILE = 512
FFN_TM = 1024
FFN_TF = 512
MOE_TM = 512
MOE_NF = 7
ATTN_HEADS_PER_STEP = 2
NEG = -1e30


def _params(sem):
    return pltpu.CompilerParams(dimension_semantics=sem, vmem_limit_bytes=VMEM_LIMIT)


def _resident(shape):
    nd = len(shape)
    return pl.BlockSpec(shape, lambda *_: (0,) * nd, pipeline_mode=pl.Buffered(1))


def _slab_specs(arrays, n_steps, step_of):
    specs = []
    for w in arrays:
        rows, cols = w.shape
        n = max(k for k in range(1, n_steps + 1) if rows % (BF16_ROWS * k) == 0)
        specs.append(pl.BlockSpec((rows // n, cols),
                                  lambda *idx, n=n: (jnp.minimum(step_of(*idx), n - 1), 0)))
    return specs


def _cast_slabs(in_refs, out_refs):
    for w_ref, wb_ref in zip(in_refs, out_refs):
        wb_ref[...] = w_ref[...].astype(BF16)


def _rmsnorm_mod(x, g, shift, scale):
    ms = jnp.mean(x * x, axis=-1, keepdims=True)
    return (x * lax.rsqrt(ms + EPS) * g) * (1.0 + scale) + shift


def _layernorm(x, g, b):
    mu = jnp.mean(x, axis=-1, keepdims=True)
    xc = x - mu
    var = jnp.mean(xc * xc, axis=-1, keepdims=True)
    return xc * lax.rsqrt(var + EPS) * g + b


def _silu(x):
    return x * jax.nn.sigmoid(x)


def _ada_kernel(c_ref, w_ref, b_ref, o_ref):
    s = _silu(c_ref[...]).astype(BF16)
    o_ref[0] = jnp.dot(s, w_ref[0].astype(BF16), preferred_element_type=F32) + b_ref[0]


def _ada(cond, ada_w, ada_b, tn=1024):
    depth, d, n = ada_w.shape
    rows = cond.shape[0]
    return pl.pallas_call(
        _ada_kernel,
        grid=(depth, n // tn),
        in_specs=[pl.BlockSpec((rows, d), lambda l, j: (0, 0)),
                  pl.BlockSpec((1, d, tn), lambda l, j: (l, 0, j)),
                  pl.BlockSpec((1, 1, tn), lambda l, j: (l, 0, j))],
        out_specs=pl.BlockSpec((1, rows, tn), lambda l, j: (l, 0, j)),
        out_shape=jax.ShapeDtypeStruct((depth, rows, n), F32),
        compiler_params=_params(("arbitrary", "arbitrary")),
        name="ada",
    )(cond, ada_w, ada_b.reshape(depth, 1, n))


def _mod_spec(layer, tm, n_lat_rows, seq, n_batch, rows_per_layer):
    def index(i, *_):
        r = jnp.where(i * tm < n_lat_rows, (i * tm) // seq, n_batch)
        return (layer * rows_per_layer + r, 0, 0)
    return index


def _pick_rows(i, n_lat_tiles, lat_ref, ctx_ref):
    return jnp.where(i < n_lat_tiles, lat_ref[...], ctx_ref[...])


def _two_source_specs(tm, d, n_lat_tiles):
    return [pl.BlockSpec((tm, d), lambda i: (jnp.minimum(i, n_lat_tiles - 1), 0)),
            pl.BlockSpec((tm, d), lambda i: (jnp.maximum(i - n_lat_tiles, 0), 0))]


def _inproj_kernel(*refs, n_lat_tiles, n_cast):
    xl_ref, xc_ref, mod_ref, g_ref, w_ref = refs[:5]
    o_ref = refs[5 + n_cast]
    x = _pick_rows(pl.program_id(0), n_lat_tiles, xl_ref, xc_ref)
    h = _rmsnorm_mod(x, g_ref[...], mod_ref[0, 0:1, :], mod_ref[0, 1:2, :])
    o_ref[...] = jnp.dot(h.astype(BF16), w_ref[...], preferred_element_type=F32).astype(o_ref.dtype)
    _cast_slabs(refs[5:5 + n_cast], refs[6 + n_cast:])


def _inproj(x_lat, x_ctx, mods, mod_index, g, w, casts, tm=MM_TILE):
    d = x_lat.shape[1]
    t = x_lat.shape[0] + x_ctx.shape[0]
    n = w.shape[1]
    n_lat_tiles = x_lat.shape[0] // tm
    cast_specs = _slab_specs(casts, t // tm, lambda i: i)
    outs = pl.pallas_call(
        functools.partial(_inproj_kernel, n_lat_tiles=n_lat_tiles, n_cast=len(casts)),
        grid=(t // tm,),
        in_specs=_two_source_specs(tm, d, n_lat_tiles) + [
            pl.BlockSpec((1, N_MOD, d), mod_index), _resident((1, d)), _resident((d, n))] + cast_specs,
        out_specs=[pl.BlockSpec((tm, n), lambda i: (i, 0))] + cast_specs,
        out_shape=[jax.ShapeDtypeStruct((t, n), BF16)] + [jax.ShapeDtypeStruct(c.shape, BF16) for c in casts],
        compiler_params=_params(("arbitrary",)),
        name="inproj0",
    )(x_lat, x_ctx, mods, g.reshape(1, d), w, *casts)
    return outs[0], outs[1:]


def _poolconv_kernel(up_ref, um_ref, un_ref, ap_ref, am_ref, an_ref, gp_ref, gm_ref, gn_ref,
                     pw_ref, ps_ref, cw_ref, cb_ref, lg_ref, lb_ref, y_ref,
                     ubuf, zbuf, zsh, cbuf, *, n_lat_tiles, seq, ctx_len):
    i = pl.program_id(0)
    tm = um_ref.shape[0]
    dp = um_ref.shape[1]
    is_lat = i < n_lat_tiles
    pos0 = jnp.where(is_lat, (i * tm) % seq, ((i - n_lat_tiles) * tm) % ctx_len)
    length = jnp.where(is_lat, seq, ctx_len)
    keep_p = jnp.where(pos0 == 0, 0.0, 1.0).astype(F32)
    keep_n = jnp.where(pos0 + tm == length, 0.0, 1.0).astype(F32)

    def glu(a_ref, g_ref):
        return a_ref[...].astype(F32) * jax.nn.sigmoid(g_ref[...].astype(F32))

    ubuf[0:HALO, :] = up_ref[...].astype(F32) * keep_p
    ubuf[HALO:HALO + tm, :] = um_ref[...].astype(F32)
    ubuf[HALO + tm:, :] = un_ref[...].astype(F32) * keep_n
    zbuf[0:HALO, :] = glu(ap_ref, gp_ref) * keep_p
    zbuf[HALO:HALO + tm, :] = glu(am_ref, gm_ref)
    zbuf[HALO + tm:, :] = glu(an_ref, gn_ref) * keep_n

    pos = pos0 + lax.broadcasted_iota(jnp.int32, (tm, 1), 0)
    gdim = dp // len(POOL_WINDOWS)
    for g, w in enumerate(POOL_WINDOWS):
        cols = slice(g * gdim, (g + 1) * gdim)
        s = ubuf[HALO - w // 2:HALO - w // 2 + tm, cols]
        for o in range(-w // 2 + 1, w // 2):
            s = s + ubuf[HALO + o:HALO + o + tm, cols]
        cnt = jnp.minimum(pos - w // 2 + w, length) - jnp.maximum(pos - w // 2, 0)
        pooled = s / cnt.astype(F32) - ubuf[HALO:HALO + tm, cols]
        mixed = jnp.dot(pooled.astype(BF16), pw_ref[g], preferred_element_type=F32)
        y_ref[:, cols] = (mixed * ps_ref[:, cols]).astype(y_ref.dtype)

    n_sh = zsh.shape[1]
    for j in range(1, SUBLANES):
        zsh[j - 1] = zbuf[j:j + n_sh, :]
    rb = 64
    first = HALO - CONV_WIDTH // 2
    for c in range(0, zbuf.shape[1], LANES):
        cols = slice(c, c + LANES)
        taps = [cw_ref[k:k + 1, cols] for k in range(CONV_WIDTH)]
        for r in range(0, tm, rb):
            acc = None
            for k in range(CONV_WIDTH):
                q, j = divmod(first + k, SUBLANES)
                rows = slice(q * SUBLANES + r, q * SUBLANES + r + rb)
                z = zbuf[rows, cols] if j == 0 else zsh[j - 1, rows, cols]
                acc = taps[k] * z if acc is None else acc + taps[k] * z
            cbuf[r:r + rb, cols] = acc
    conv = _layernorm(cbuf[...] + cb_ref[...], lg_ref[...], lb_ref[...])
    y_ref[:, dp:] = _silu(conv).astype(y_ref.dtype)


def _poolconv(p, pool_w, pool_scale, conv_w, conv_b, ln_g, ln_b, n_lat_rows, seq, ctx_len,
              tm=ROW_TILE):
    t = p.shape[0]
    dp = pool_scale.shape[0]
    dc = conv_b.shape[0]
    hb = tm // HALO
    last_hb = t // HALO - 1

    def main(c):
        return pl.BlockSpec((tm, dp), lambda i: (i, c))

    def prev(c):
        return pl.BlockSpec((HALO, dp), lambda i: (jnp.maximum(i * hb - 1, 0), c))

    def nxt(c):
        return pl.BlockSpec((HALO, dp), lambda i: (jnp.minimum((i + 1) * hb, last_hb), c))

    kern = functools.partial(_poolconv_kernel, n_lat_tiles=n_lat_rows // tm, seq=seq, ctx_len=ctx_len)
    return pl.pallas_call(
        kern,
        grid=(t // tm,),
        in_specs=[prev(0), main(0), nxt(0), prev(1), main(1), nxt(1), prev(2), main(2), nxt(2),
                  _resident(pool_w.shape), _resident((1, dp)), _resident(conv_w.shape),
                  _resident((1, dc)), _resident((1, dc)), _resident((1, dc))],
        out_specs=pl.BlockSpec((tm, dp + dc), lambda i: (i, 0)),
        out_shape=jax.ShapeDtypeStruct((t, dp + dc), BF16),
        scratch_shapes=[pltpu.VMEM((tm + 2 * HALO, dp), F32),
                        pltpu.VMEM((tm + 2 * HALO, dc), F32),
                        pltpu.VMEM((SUBLANES - 1, tm + 2 * HALO - SUBLANES, dc), F32),
                        pltpu.VMEM((tm, dc), F32)],
        compiler_params=_params(("arbitrary",)),
        name="poolconv",
    )(p, p, p, p, p, p, p, p, p, pool_w, pool_scale.reshape(1, dp), conv_w,
      conv_b.reshape(1, dc), ln_g.reshape(1, dc), ln_b.reshape(1, dc))


def _outproj_kernel(*refs, n_in, n_res, n_lat_tiles, router):
    y_refs = refs[:n_in]
    w_ref = refs[n_in]
    x_refs = refs[n_in + 1:n_in + 1 + n_res]
    mod_ref, g_ref = refs[n_in + 1 + n_res:n_in + 3 + n_res]
    rest = refs[n_in + 3 + n_res:]
    if router:
        rw_ref, rb_ref, xo_ref, ho_ref, r_ref = rest
    else:
        xo_ref, ho_ref = rest
    o = None
    k0 = 0
    for y_ref in y_refs:
        kk = y_ref.shape[1]
        part = jnp.dot(y_ref[...], w_ref[k0:k0 + kk, :], preferred_element_type=F32)
        o = part if o is None else o + part
        k0 += kk
    if n_res == 2:
        x = _pick_rows(pl.program_id(0), n_lat_tiles, x_refs[0], x_refs[1])
    else:
        x = x_refs[0][...]
    x1 = x + mod_ref[0, 2:3, :] * o
    xo_ref[...] = x1
    h = _rmsnorm_mod(x1, g_ref[...], mod_ref[0, 3:4, :], mod_ref[0, 4:5, :])
    ho_ref[...] = h.astype(ho_ref.dtype)
    if router:
        logits = jnp.dot(h.astype(BF16), rw_ref[...], preferred_element_type=F32) + rb_ref[...]
        lane = lax.broadcasted_iota(jnp.int32, logits.shape, 1)
        m1 = jnp.max(logits, axis=-1, keepdims=True)
        i1 = jnp.min(jnp.where(logits == m1, lane, LANES), axis=-1, keepdims=True)
        rest_l = jnp.where(lane == i1, NEG * 2, logits)
        m2 = jnp.max(rest_l, axis=-1, keepdims=True)
        i2 = jnp.min(jnp.where(rest_l == m2, lane, LANES), axis=-1, keepdims=True)
        e2 = jnp.exp(m2 - m1)
        p1 = 1.0 / (1.0 + e2)
        p2 = e2 / (1.0 + e2)
        r = jnp.where(lane == 0, i1.astype(F32), 0.0)
        r = jnp.where(lane == 1, i2.astype(F32), r)
        r = jnp.where(lane == 2, p1, r)
        r = jnp.where(lane == 3, p2, r)
        r_ref[...] = r


def _outproj(ys, w, res, mods, mod_index, g, n_rows, h_dtype, router=None, tm=MM_TILE):
    d = w.shape[1]
    in_specs = [pl.BlockSpec((tm, y.shape[1]), lambda i: (i, 0)) for y in ys]
    in_specs.append(_resident(w.shape))
    n_lat_tiles = res[0].shape[0] // tm
    if len(res) == 2:
        in_specs += _two_source_specs(tm, d, n_lat_tiles)
    else:
        in_specs.append(pl.BlockSpec((tm, d), lambda i: (i, 0)))
    in_specs += [pl.BlockSpec((1, N_MOD, d), mod_index), _resident((1, d))]
    args = list(ys) + [w] + list(res) + [mods, g.reshape(1, d)]
    out_specs = [pl.BlockSpec((tm, d), lambda i: (i, 0)), pl.BlockSpec((tm, d), lambda i: (i, 0))]
    out_shape = [jax.ShapeDtypeStruct((n_rows, d), F32), jax.ShapeDtypeStruct((n_rows, d), h_dtype)]
    if router is not None:
        rw, rb = router
        in_specs += [_resident(rw.shape), _resident(rb.shape)]
        args += [rw, rb]
        out_specs.append(pl.BlockSpec((tm, LANES), lambda i: (i, 0)))
        out_shape.append(jax.ShapeDtypeStruct((n_rows, LANES), F32))
    kern = functools.partial(_outproj_kernel, n_in=len(ys), n_res=len(res), n_lat_tiles=n_lat_tiles,
                             router=router is not None)
    return pl.pallas_call(
        kern,
        grid=(n_rows // tm,),
        in_specs=in_specs,
        out_specs=out_specs,
        out_shape=out_shape,
        compiler_params=_params(("arbitrary",)),
        name="outproj_router" if router is not None else "outproj",
    )(*args)


def _ffn_kernel(h_ref, wg_ref, wu_ref, wd_ref, x_hbm, mod_ref, o_ref, xres, sem):
    i = pl.program_id(0)
    f = pl.program_id(1)
    tm = o_ref.shape[0]
    residual = pltpu.make_async_copy(x_hbm.at[pl.ds(pl.multiple_of(i * tm, tm), tm)], xres, sem)

    @pl.when(f == 0)
    def _():
        residual.start()
        o_ref[...] = jnp.zeros(o_ref.shape, F32)

    h = h_ref[...]
    a = jnp.dot(h, wg_ref[...], preferred_element_type=F32)
    b = jnp.dot(h, wu_ref[...], preferred_element_type=F32)
    o_ref[...] += jnp.dot((_silu(a) * b).astype(BF16), wd_ref[...], preferred_element_type=F32)

    @pl.when(f == pl.num_programs(1) - 1)
    def _():
        residual.wait()
        o_ref[...] = xres[...] + mod_ref[0, 5:6, :] * o_ref[...]


def _ffn(h, wg, wu, wd, xs, mods, mod_index, tm=FFN_TM, tf=FFN_TF):
    t, d = xs.shape
    ff = wg.shape[1]
    return pl.pallas_call(
        _ffn_kernel,
        grid=(t // tm, ff // tf),
        in_specs=[pl.BlockSpec((tm, d), lambda i, f: (i, 0)),
                  pl.BlockSpec((d, tf), lambda i, f: (0, f)),
                  pl.BlockSpec((d, tf), lambda i, f: (0, f)),
                  pl.BlockSpec((tf, d), lambda i, f: (f, 0)),
                  pl.BlockSpec(memory_space=pl.ANY),
                  pl.BlockSpec((1, N_MOD, d), mod_index)],
        out_specs=pl.BlockSpec((tm, d), lambda i, f: (i, 0)),
        out_shape=jax.ShapeDtypeStruct((t, d), F32),
        scratch_shapes=[pltpu.VMEM((tm, d), F32), pltpu.SemaphoreType.DMA],
        compiler_params=_params(("arbitrary", "arbitrary")),
        name="ffn",
    )(h, wg, wu, wd, xs, mods)


def _rope(x, c, s):
    lane = lax.broadcasted_iota(jnp.int32, x.shape, 1)
    partner = jnp.where(lane % 32 < 16, pltpu.roll(x, LANES - 16, 1), pltpu.roll(x, 16, 1))
    return x * c + partner * s


def _mla_kernel(x_ref, mod_ref, g_ref, wlora_ref, wkr_ref, wsgu_ref, qg_ref, wq_ref, kvg_ref, wkv_ref,
                cos_ref, sin_ref, slg_ref, slb_ref, sw_ref, sb_ref,
                q_ref, k_ref, v_ref, sg_ref, *, n_lat_tiles):
    i = pl.program_id(0)
    tm = x_ref.shape[0]
    h = _rmsnorm_mod(x_ref[...], g_ref[...], mod_ref[0, 0:1, :], mod_ref[0, 1:2, :]).astype(BF16)
    p_lora = jnp.dot(h, wlora_ref[...], preferred_element_type=F32)
    p_sgu = jnp.dot(h, wsgu_ref[...], preferred_element_type=F32)
    d_sgu = slg_ref.shape[1]
    cos = cos_ref[...]
    sin = sin_ref[...]

    def rms(x, g):
        return x * lax.rsqrt(jnp.mean(x * x, axis=-1, keepdims=True) + EPS) * g

    ckv = rms(p_lora[:, Q_LORA:], kvg_ref[...]).astype(BF16)
    kv = jnp.dot(ckv, wkv_ref[...], preferred_element_type=F32)
    kr = _rope(jnp.dot(h, wkr_ref[...], preferred_element_type=F32), cos, sin).astype(BF16)
    nk = MLA_HEADS * QK_NOPE
    for hd in range(MLA_HEADS):
        k_ref[:, hd * HEAD_PAD:hd * HEAD_PAD + QK_NOPE] = kv[:, hd * QK_NOPE:(hd + 1) * QK_NOPE].astype(BF16)
        k_ref[:, hd * HEAD_PAD + QK_NOPE:(hd + 1) * HEAD_PAD] = kr
        v_ref[:, hd * HEAD_PAD:hd * HEAD_PAD + V_HEAD] = kv[:, nk + hd * V_HEAD:nk + (hd + 1) * V_HEAD].astype(BF16)
        v_ref[:, hd * HEAD_PAD + V_HEAD:(hd + 1) * HEAD_PAD] = jnp.ones((tm, HEAD_PAD - V_HEAD), BF16)

    @pl.when(i < n_lat_tiles)
    def _():
        cq = rms(p_lora[:, :Q_LORA], qg_ref[...]).astype(BF16)
        q = jnp.dot(cq, wq_ref[...], preferred_element_type=F32) * (SM_SCALE * LOG2E)
        for hd in range(MLA_HEADS):
            b0 = hd * HEAD_PAD
            q_ref[:, b0:b0 + QK_NOPE] = q[:, b0:b0 + QK_NOPE].astype(BF16)
            q_ref[:, b0 + QK_NOPE:b0 + HEAD_PAD] = _rope(q[:, b0 + QK_NOPE:b0 + HEAD_PAD], cos, sin).astype(BF16)

        zu = jax.nn.gelu(p_sgu[:, :d_sgu])
        zv = _layernorm(jax.nn.gelu(p_sgu[:, d_sgu:]), slg_ref[...], slb_ref[...]).astype(BF16)
        hdim = d_sgu // SGU_HEADS
        for ck in range(tm // CHUNK):
            rows = slice(ck * CHUNK, (ck + 1) * CHUNK)
            for hd in range(SGU_HEADS):
                cols = slice(hd * hdim, (hd + 1) * hdim)
                mixed = jnp.dot(sw_ref[hd], zv[rows, cols], preferred_element_type=F32) + sb_ref[:, cols]
                sg_ref[rows, cols] = (zu[rows, cols] * mixed).astype(BF16)


def _mla_proj(xs, mods, mod_index, g, w_lora, w_kr, w_sgu, qg, wq, kvg, wkv, cos_t, sin_t, slg, slb, sw, sb,
              n_lat_rows, seq, tm=MM_TILE):
    t, d = xs.shape
    n_lat_tiles = n_lat_rows // tm
    seq_tiles = seq // tm
    d_sgu = slg.shape[0]

    def lat(i):
        return (jnp.minimum(i, n_lat_tiles - 1), 0)

    def table(i):
        return (jnp.where(i < n_lat_tiles, i % seq_tiles, seq_tiles), 0)

    kern = functools.partial(_mla_kernel, n_lat_tiles=n_lat_tiles)
    return pl.pallas_call(
        kern,
        grid=(t // tm,),
        in_specs=[pl.BlockSpec((tm, d), lambda i: (i, 0)),
                  pl.BlockSpec((1, N_MOD, d), mod_index),
                  _resident((1, d)), _resident(w_lora.shape), _resident(w_kr.shape), _resident(w_sgu.shape),
                  _resident((1, Q_LORA)), _resident(wq.shape),
                  _resident((1, KV_LORA)), _resident(wkv.shape),
                  pl.BlockSpec((tm, LANES), table), pl.BlockSpec((tm, LANES), table),
                  _resident((1, d_sgu)), _resident((1, d_sgu)), _resident(sw.shape), _resident(sb.shape)],
        out_specs=[pl.BlockSpec((tm, MLA_HEADS * HEAD_PAD), lat),
                   pl.BlockSpec((tm, MLA_HEADS * HEAD_PAD), lambda i: (i, 0)),
                   pl.BlockSpec((tm, MLA_HEADS * HEAD_PAD), lambda i: (i, 0)),
                   pl.BlockSpec((tm, d_sgu), lat)],
        out_shape=[jax.ShapeDtypeStruct((n_lat_rows, MLA_HEADS * HEAD_PAD), BF16),
                   jax.ShapeDtypeStruct((t, MLA_HEADS * HEAD_PAD), BF16),
                   jax.ShapeDtypeStruct((t, MLA_HEADS * HEAD_PAD), BF16),
                   jax.ShapeDtypeStruct((n_lat_rows, d_sgu), BF16)],
        compiler_params=_params(("arbitrary",)),
        name="mla_proj",
    )(xs, mods, g.reshape(1, d), w_lora, w_kr, w_sgu, qg.reshape(1, -1), wq, kvg.reshape(1, -1), wkv,
      cos_t, sin_t, slg.reshape(1, -1), slb.reshape(1, -1), sw, sb)


def _attn_kernel(*refs, n_cast):
    q_ref, kc_ref, kl_ref, vc_ref, vl_ref = refs[:5]
    cast_in = refs[5:5 + n_cast]
    o_ref = refs[5 + n_cast]
    cast_out = refs[6 + n_cast:]
    dn = (((1,), (1,)), ((), ()))

    def scores(hd):
        qk = slice(hd * HEAD_PAD, (hd + 1) * HEAD_PAD)
        q = q_ref[:, qk]
        return (lax.dot_general(q, kc_ref[:, qk], dn, preferred_element_type=F32),
                lax.dot_general(q, kl_ref[:, qk], dn, preferred_element_type=F32))

    def finish(hd, sc, sl):
        vv = slice(hd * HEAD_PAD, (hd + 1) * HEAD_PAD)
        m = jnp.maximum(jnp.max(sc, axis=-1, keepdims=True), jnp.max(sl, axis=-1, keepdims=True))
        pc = jnp.exp2((sc - m).astype(BF16))
        pq = jnp.exp2((sl - m).astype(BF16))
        o = (jnp.dot(pc, vc_ref[:, vv], preferred_element_type=F32)
             + jnp.dot(pq, vl_ref[:, vv], preferred_element_type=F32))
        out = slice(hd * V_HEAD, (hd + 1) * V_HEAD)
        o_ref[:, out] = (o[:, :V_HEAD] / o[:, V_HEAD:V_HEAD + 1]).astype(o_ref.dtype)

    s_next = scores(0)
    for hd in range(ATTN_HEADS_PER_STEP):
        s_cur = s_next
        if hd + 1 < ATTN_HEADS_PER_STEP:
            s_next = scores(hd + 1)
        finish(hd, *s_cur)
    _cast_slabs(cast_in, cast_out)


def _attention(q, k, v, n_batch, seq, ctx_len, casts, tq=ROW_TILE):
    n_lat_rows = n_batch * seq
    qb = seq // tq
    ctx_blk0 = n_lat_rows // ctx_len
    hs = ATTN_HEADS_PER_STEP
    hp = MLA_HEADS // hs
    n_steps = n_batch * hp * qb

    cast_specs = _slab_specs(casts, n_steps, lambda b, h, j: (b * hp + h) * qb + j)
    outs = pl.pallas_call(
        functools.partial(_attn_kernel, n_cast=len(casts)),
        grid=(n_batch, hp, qb),
        in_specs=[pl.BlockSpec((tq, hs * HEAD_PAD), lambda b, h, j: (b * qb + j, h)),
                  pl.BlockSpec((ctx_len, hs * HEAD_PAD), lambda b, h, j: (ctx_blk0 + b, h)),
                  pl.BlockSpec((seq, hs * HEAD_PAD), lambda b, h, j: (b, h)),
                  pl.BlockSpec((ctx_len, hs * HEAD_PAD), lambda b, h, j: (ctx_blk0 + b, h)),
                  pl.BlockSpec((seq, hs * HEAD_PAD), lambda b, h, j: (b, h))] + cast_specs,
        out_specs=[pl.BlockSpec((tq, hs * V_HEAD), lambda b, h, j: (b * qb + j, h))] + cast_specs,
        out_shape=[jax.ShapeDtypeStruct((n_lat_rows, MLA_HEADS * V_HEAD), BF16)]
        + [jax.ShapeDtypeStruct(w.shape, BF16) for w in casts],
        compiler_params=_params(("arbitrary", "arbitrary", "arbitrary")),
        name="attention",
    )(q, k, k, v, v, *casts)
    return outs[0], outs[1:]


SRC_BITS = 15
ITEM_RANGE = 1 << 16


def _moe_kernel(te_ref, nact_ref, rows_ref, route_ref, h_hbm, wg_ref, wu_ref, wd_ref, y_hbm,
                xbuf, xb, acc, gsem, ssem):
    i = pl.program_id(0)
    f = pl.program_id(1)
    tm = xbuf.shape[0]
    chunk = tm // (MOE_NF + 1)
    nact = nact_ref[0]
    active = i < nact
    slot = i % 2
    dump0 = y_hbm.shape[0] - tm

    def gather_copy(tile, r):
        src = route_ref[(tile + 1) * tm + r] & ((1 << SRC_BITS) - 1)
        return pltpu.make_async_copy(h_hbm.at[pl.ds(src, 1)], xbuf.at[pl.ds(r, 1)], gsem)

    def scatter_copy(tile, r):
        dst = route_ref[(tile + 1) * tm + r] >> SRC_BITS
        return pltpu.make_async_copy(acc.at[tile % 2, pl.ds(r, 1)], y_hbm.at[pl.ds(dst, 1)], ssem)

    def wait_gather():
        pltpu.make_async_copy(h_hbm.at[pl.ds(0, tm)], xbuf, gsem).wait()

    def wait_scatter():
        pltpu.make_async_copy(acc.at[0], y_hbm.at[pl.ds(0, tm)], ssem).wait()

    def start_all(copy, tile):
        def body(r, c):
            copy(tile, r).start()
            return c
        lax.fori_loop(0, tm, body, 0)

    @pl.when((i == 0) & (f == 0))
    def _():
        acc[1] = jnp.zeros(acc.shape[1:], F32)
        fill = pltpu.make_async_copy(acc.at[1], y_hbm.at[pl.ds(dump0, tm)], ssem)
        fill.start()
        fill.wait()
        start_all(gather_copy, 0)

    @pl.when((f == 0) & (i <= nact))
    def _():
        wait_gather()

        @pl.when(i >= 1)
        def _():
            wait_scatter()

    def issue_chunk(c):
        for j in range(chunk):
            r = c * chunk + j
            gather_copy(i + 1, r).start(priority=1)
            scatter_copy(i - 1, r).start(priority=1)

    @pl.when((f == 0) & active)
    def _():
        xb[...] = xbuf[...].astype(BF16)
        acc[slot] = jnp.zeros(acc.shape[1:], F32)
        issue_chunk(0)

    @pl.when((f == 0) & (i == nact) & (i >= 1))
    def _():
        start_all(scatter_copy, i - 1)
        wait_scatter()

    def expert_step(m):
        issue_chunk(f + 1)
        x = xb[0:m, :]
        a = jnp.dot(x, wg_ref[0], preferred_element_type=F32)
        b = jnp.dot(x, wu_ref[0], preferred_element_type=F32)
        acc[slot, 0:m, :] += jnp.dot((_silu(a) * b).astype(BF16), wd_ref[0], preferred_element_type=F32)

    more_than_half = rows_ref[i] > tm // 2

    @pl.when(active & more_than_half)
    def _():
        expert_step(tm)

    @pl.when(active & jnp.logical_not(more_than_half))
    def _():
        expert_step(tm // 2)


def _moe(h, wg, wu, wd, tile_expert, n_active, tile_rows, route, n_out_rows, tm=MOE_TM):
    d = h.shape[1]
    ff = wg.shape[2]
    n_tiles = tile_expert.shape[0]
    nf = MOE_NF
    tf = ff // nf

    def fidx(i, f, nact):
        return jnp.where(i < nact[0], f, nf - 1)

    grid_spec = pltpu.PrefetchScalarGridSpec(
        num_scalar_prefetch=4,
        grid=(n_tiles, nf),
        in_specs=[pl.BlockSpec(memory_space=pl.ANY),
                  pl.BlockSpec((1, d, tf), lambda i, f, te, nact, rows, route: (te[i], 0, fidx(i, f, nact))),
                  pl.BlockSpec((1, d, tf), lambda i, f, te, nact, rows, route: (te[i], 0, fidx(i, f, nact))),
                  pl.BlockSpec((1, tf, d), lambda i, f, te, nact, rows, route: (te[i], fidx(i, f, nact), 0))],
        out_specs=pl.BlockSpec(memory_space=pl.ANY),
        scratch_shapes=[pltpu.VMEM((tm, d), F32), pltpu.VMEM((tm, d), BF16), pltpu.VMEM((2, tm, d), F32),
                        pltpu.SemaphoreType.DMA, pltpu.SemaphoreType.DMA],
    )
    return pl.pallas_call(
        _moe_kernel,
        grid_spec=grid_spec,
        out_shape=jax.ShapeDtypeStruct((n_out_rows, d), F32),
        compiler_params=_params(("arbitrary", "arbitrary")),
        name="moe",
    )(tile_expert, n_active, tile_rows, route, h, wg, wu, wd)


def _route_plan(r, n_tok, tm):
    n_assign = 2 * n_tok
    n_tiles = n_assign // tm + N_EXPERTS
    assert n_tiles * tm <= ITEM_RANGE
    e_flat = jnp.concatenate([r[:, 0], r[:, 1]]).astype(jnp.int32)
    experts = jnp.arange(N_EXPERTS, dtype=jnp.int32)
    counts = jnp.sum(e_flat[:, None] == experts[None, :], axis=0).astype(jnp.int32)
    tiles_per = (counts + tm - 1) // tm
    tend = jnp.cumsum(tiles_per)
    n_active = tend[-1]
    j = jnp.arange(n_tiles, dtype=jnp.int32)
    te = jnp.minimum(jnp.sum(j[:, None] >= tend[None, :], axis=1), N_EXPERTS - 1).astype(jnp.int32)
    last = jnp.sum(jnp.where(j == n_active - 1, te, 0))
    te = jnp.where(j < n_active, te, last)
    mine = te[:, None] == experts[None, :]
    first_tile = jnp.sum(jnp.where(mine, (tend - tiles_per)[None, :], 0), axis=1)
    group_rows = jnp.sum(jnp.where(mine, counts[None, :], 0), axis=1)
    tile_rows = jnp.where(j < n_active, jnp.clip(group_rows - (j - first_tile) * tm, 0, tm), 0).astype(jnp.int32)
    pad_id = jnp.arange(tm, dtype=jnp.int32)
    pad_on = pad_id[None, :] < (tiles_per * tm - counts)[:, None]
    pad_key = jnp.where(pad_on, 2 * experts[:, None] + 1, 2 * N_EXPERTS).reshape(-1)
    keys = jnp.concatenate([2 * e_flat, pad_key])
    row = jnp.arange(n_tiles * tm, dtype=jnp.int32)
    item = jnp.sort(keys * ITEM_RANGE + row) % ITEM_RANGE
    real = item < n_assign
    src = jnp.where(real, item % n_tok, 0)
    dst = jnp.where(real, item, n_assign + row % tm)
    body = src | (dst << SRC_BITS)
    edge = (jnp.arange(tm, dtype=jnp.int32) + n_assign) << SRC_BITS
    route = jnp.concatenate([edge, body, edge]).astype(jnp.int32)
    return te, n_active.reshape(1).astype(jnp.int32), tile_rows, route


def _combine_kernel(x_ref, y0_ref, y1_ref, r_ref, mod_ref, g_ref, o_ref):
    r = r_ref[...]
    y = r[:, 2:3] * y0_ref[...] + r[:, 3:4] * y1_ref[...]
    x = x_ref[...] + mod_ref[0, 5:6, :] * y
    ms = jnp.mean(x * x, axis=-1, keepdims=True)
    o_ref[...] = x * lax.rsqrt(ms + EPS) * g_ref[...]


def _combine(xs, y, r, mods, mod_index, g, tm=MM_TILE):
    t, d = xs.shape
    nb = t // tm
    return pl.pallas_call(
        _combine_kernel,
        grid=(nb,),
        in_specs=[pl.BlockSpec((tm, d), lambda i: (i, 0)),
                  pl.BlockSpec((tm, d), lambda i: (i, 0)),
                  pl.BlockSpec((tm, d), lambda i: (nb + i, 0)),
                  pl.BlockSpec((tm, LANES), lambda i: (i, 0)),
                  pl.BlockSpec((1, N_MOD, d), mod_index),
                  _resident((1, d))],
        out_specs=pl.BlockSpec((tm, d), lambda i: (i, 0)),
        out_shape=jax.ShapeDtypeStruct((t, d), F32),
        compiler_params=_params(("arbitrary",)),
        name="combine",
    )(xs, y, y, r, mods, g.reshape(1, d))


def _rope_tables(seq, extra_rows):
    rows = seq // GRID_W
    row = np.repeat(np.arange(rows), GRID_W).astype(np.float32)
    col = np.tile(np.arange(GRID_W), rows).astype(np.float32)
    inv = np.float32(ROPE_THETA) ** (-np.arange(ROPE_FREQS, dtype=np.float32) / np.float32(ROPE_FREQS))
    ar = row[:, None] * inv
    ac = col[:, None] * inv
    cos = np.ones((seq + extra_rows, LANES), np.float32)
    sin = np.zeros((seq + extra_rows, LANES), np.float32)
    cos[:seq, :QK_ROPE] = np.concatenate([np.cos(ar), np.cos(ar), np.cos(ac), np.cos(ac)], axis=1)
    sin[:seq, :QK_ROPE] = np.concatenate([-np.sin(ar), np.sin(ar), -np.sin(ac), np.sin(ac)], axis=1)
    return jnp.asarray(cos), jnp.asarray(sin)


def kernel(x, c, ctx, c_ctx, ada_w, ada_b, norm1_g, norm2_g, e_w_in, e_pool_w, e_pool_scale, e_conv_w, e_conv_b, e_conv_ln_g, e_conv_ln_b, e_w_out, e_ffn_w_gate, e_ffn_w_up, e_ffn_w_down, o_w_in, o_q_norm_g, o_w_qb, o_kv_norm_g, o_w_kvb, o_sgu_ln_g, o_sgu_ln_b, o_sgu_w, o_sgu_b, o_w_out, o_router_w, o_router_b, o_exp_w_gate, o_exp_w_up, o_exp_w_down, final_norm_g):
    n_batch, seq, d = x.shape
    ctx_len = ctx.shape[1]
    n_lat = n_batch * seq
    n_ctx = n_batch * ctx_len
    assert ctx_len % ROW_TILE == 0 and seq % FFN_TM == 0 and n_ctx % FFN_TM == 0
    assert ada_w.shape[0] == 2 and e_w_in.shape[0] == 1 and o_w_in.shape[0] == 1

    rows_per_layer = 8 * ((n_batch + 1 + 7) // 8)
    cond = jnp.zeros((rows_per_layer, d), F32).at[:n_batch].set(c).at[n_batch].set(c_ctx)
    mods = _ada(cond, ada_w, ada_b).reshape(2 * rows_per_layer, N_MOD, d)

    def mod_index(layer, tm):
        return _mod_spec(layer, tm, n_lat, seq, n_batch, rows_per_layer)

    x_lat = x.reshape(n_lat, d)
    x_ctx = ctx.reshape(n_ctx, d)

    p, (fg, fu, fd) = _inproj(x_lat, x_ctx, mods, mod_index(0, MM_TILE), norm1_g[0], e_w_in[0].astype(BF16),
                              [e_ffn_w_gate[0], e_ffn_w_up[0], e_ffn_w_down[0]])
    y = _poolconv(p, e_pool_w[0].astype(BF16), e_pool_scale[0], e_conv_w[0], e_conv_b[0],
                  e_conv_ln_g[0], e_conv_ln_b[0], n_lat, seq, ctx_len)
    x1, h2 = _outproj([y], e_w_out[0].astype(BF16), (x_lat, x_ctx), mods, mod_index(0, MM_TILE), norm2_g[0],
                      n_lat + n_ctx, BF16)
    x2 = _ffn(h2, fg, fu, fd, x1, mods, mod_index(0, FFN_TM))

    w_in = o_w_in[0]
    c_kr = Q_LORA + KV_LORA
    c_u = c_kr + QK_ROPE
    w_lora = w_in[:, :c_kr].astype(BF16)
    w_kr = jnp.pad(w_in[:, c_kr:c_u], ((0, 0), (0, LANES - QK_ROPE))).astype(BF16)
    w_sgu = w_in[:, c_u:].astype(BF16)
    wq = o_w_qb[0].reshape(Q_LORA, MLA_HEADS, QK_NOPE + QK_ROPE)
    wq = jnp.pad(wq, ((0, 0), (0, 0), (0, HEAD_PAD - QK_NOPE - QK_ROPE)))
    wq = wq.reshape(Q_LORA, MLA_HEADS * HEAD_PAD).astype(BF16)
    wkv = o_w_kvb[0].reshape(KV_LORA, MLA_HEADS, QK_NOPE + V_HEAD)
    wkv = jnp.concatenate([wkv[:, :, :QK_NOPE].reshape(KV_LORA, -1),
                           wkv[:, :, QK_NOPE:].reshape(KV_LORA, -1)], axis=1).astype(BF16)
    cos_t, sin_t = _rope_tables(seq, MM_TILE)
    d_sgu = o_sgu_ln_g.shape[1]
    sgu_bias = jnp.repeat(o_sgu_b[0].T, d_sgu // SGU_HEADS, axis=1)
    q, k, v, sg = _mla_proj(x2, mods, mod_index(1, MM_TILE), norm1_g[1], w_lora, w_kr, w_sgu,
                            o_q_norm_g[0], wq, o_kv_norm_g[0], wkv, cos_t, sin_t,
                            o_sgu_ln_g[0], o_sgu_ln_b[0], o_sgu_w[0].astype(BF16), sgu_bias,
                            n_lat, seq)
    experts = (o_exp_w_gate[0], o_exp_w_up[0], o_exp_w_down[0])
    attn, experts_bf16 = _attention(q, k, v, n_batch, seq, ctx_len,
                                    [w.reshape(-1, w.shape[-1]) for w in experts])
    wg, wu, wd = (wb.reshape(w.shape) for wb, w in zip(experts_bf16, experts))

    rw = jnp.pad(o_router_w[0], ((0, 0), (0, LANES - N_EXPERTS))).astype(BF16)
    rb = jnp.concatenate([o_router_b[0], jnp.full((LANES - N_EXPERTS,), NEG, F32)]).reshape(1, LANES)
    x3, h3, r = _outproj([attn, sg], o_w_out[0].astype(BF16), (x2,), mods, mod_index(1, MM_TILE),
                         norm2_g[1], n_lat, F32, router=(rw, rb))

    te, n_active, tile_rows, route = _route_plan(r, n_lat, MOE_TM)
    ys = _moe(h3, wg, wu, wd, te, n_active, tile_rows, route, 2 * n_lat + MOE_TM)
    out = _combine(x3, ys, r, mods, mod_index(1, MM_TILE), final_norm_g)
    return out.reshape(n_batch, seq, d)
```
